```python
import functools
import jax, jax.numpy as jnp
from jax import lax
import numpy as np

D_MODEL = 4096
BATCH = 4
SEQ = 2048
DEPTH = 1
DEC_BATCH = 128
DEC_SEQ = 1
PAST_LEN = 2048
PAGE_SIZE = 128

FOX_WIDTH = D_MODEL // 2
HEAD_DIM = 128
H_FOX = FOX_WIDTH // HEAD_DIM
Q_BLOCK = 128
ATTN_SCALE = HEAD_DIM ** -0.5
FOX_FGATE_BIAS_INIT = 3.0
HG_WIDTH = D_MODEL // 2
HG_DK = 128
H_HG = HG_WIDTH // HG_DK
HG_DV = HG_WIDTH // H_HG
HG_CHUNK = 64
N_EXPERTS = 32
TOP_K = 4
D_EXPERT = D_MODEL
SWIGLU_LIMIT = 7.0
SWIGLU_ALPHA = 1.702
MOE_BLOCK_ROWS = 128
PLE_DIM = 256
RMS_EPS = 1e-6
IN_SIZES = (FOX_WIDTH, FOX_WIDTH, FOX_WIDTH, H_FOX, HG_WIDTH, HG_WIDTH, HG_WIDTH, HG_WIDTH, D_MODEL, D_MODEL)
IN_COLS = sum(IN_SIZES)

kernel_name = 'fox_hgrn2_moe_hybrid_step'


def rms_norm(x, gain):
    xf = x.astype(jnp.float32)
    y = xf * lax.rsqrt(jnp.mean(xf * xf, axis=-1, keepdims=True) + RMS_EPS)
    return (y * gain.astype(jnp.float32)).astype(x.dtype)


def split_columns(proj):
    offsets, acc = [], 0
    for size in IN_SIZES[:-1]:
        acc += size
        offsets.append(acc)
    return jnp.split(proj, offsets, axis=-1)


def fox_prompt_attention(q, k, v, logf):
    B, T, H, Dh = q.shape
    qb_size = min(Q_BLOCK, T)
    nb = T // qb_size
    c = jnp.cumsum(logf, axis=1).transpose(0, 2, 1)
    pos = jnp.arange(T, dtype=jnp.int32)
    q_blocks = q.reshape(B, nb, qb_size, H, Dh).transpose(1, 0, 2, 3, 4)
    c_blocks = c.reshape(B, H, nb, qb_size).transpose(2, 0, 1, 3)
    qpos = pos.reshape(nb, qb_size)

    def block(args):
        qi, ci, pi = args
        s = jnp.einsum('bqhd,bkhd->bhqk', qi, k).astype(jnp.float32) * ATTN_SCALE
        s = s + (ci[..., :, None] - c[:, :, None, :])
        s = jnp.where(pi[:, None] >= pos[None, :], s, -jnp.inf)
        probs = jax.nn.softmax(s, axis=-1)
        return jnp.einsum('bhqk,bkhd->bqhd', probs.astype(v.dtype), v)

    o = lax.map(block, (q_blocks, c_blocks, qpos))
    return o.transpose(1, 0, 2, 3, 4).reshape(B, T, H, Dh)


def fox_paged_attention(q, k_new, v_new, logf_new, cache_k, cache_v, cache_logf, page_table, layer):
    B, T, H, Dh = q.shape
    n_pages = page_table.shape[1]
    past = n_pages * PAGE_SIZE
    logf_past = cache_logf[layer, page_table].reshape(B, past, H).astype(jnp.float32)
    c = jnp.cumsum(jnp.concatenate([logf_past, logf_new], axis=1), axis=1).transpose(0, 2, 1)
    bias = c[:, :, past:, None] - c[:, :, None, :]
    pages_by_logical = page_table.T

    def page_scores(pages):
        return jnp.einsum('bqhd,bkhd->bhqk', q, cache_k[layer, pages]).astype(jnp.float32)

    s_past = lax.map(page_scores, pages_by_logical)
    s_past = s_past.transpose(1, 2, 3, 0, 4).reshape(B, H, T, past) * ATTN_SCALE + bias[..., :past]
    s_new = jnp.einsum('bqhd,bkhd->bhqk', q, k_new).astype(jnp.float32) * ATTN_SCALE + bias[..., past:]
    s_new = jnp.where(jnp.tril(jnp.ones((T, T), dtype=bool)), s_new, -jnp.inf)
    probs = jax.nn.softmax(jnp.concatenate([s_past, s_new], axis=-1), axis=-1)
    p_past = probs[..., :past].reshape(B, H, T, n_pages, PAGE_SIZE).transpose(3, 0, 1, 2, 4).astype(v_new.dtype)

    def page_values(args):
        pp, pages = args
        return jnp.einsum('bhqk,bkhd->bqhd', pp, cache_v[layer, pages])

    o_past = jnp.sum(lax.map(page_values, (p_past, pages_by_logical)), axis=0)
    o_new = jnp.einsum('bhqk,bkhd->bqhd', probs[..., past:].astype(v_new.dtype), v_new)
    return o_past + o_new


def hgrn2_chunked(q, k, log_f, v, s0):
    B, T, H, DK = q.shape
    DV = v.shape[-1]
    C = HG_CHUNK if T % HG_CHUNK == 0 else T
    n = T // C

    def to_chunks(a):
        return a.astype(jnp.float32).reshape(B, n, C, H, a.shape[-1]).transpose(1, 0, 3, 2, 4)

    causal = jnp.tril(jnp.ones((C, C), dtype=bool))[None, None, :, :, None]

    def step(S, inp):
        qc, kc, gc, vc = inp
        b = jnp.cumsum(gc, axis=2)
        decay = jnp.exp(jnp.where(causal, b[:, :, :, None, :] - b[:, :, None, :, :], -jnp.inf))
        attn = jnp.einsum('bhtd,bhsd,bhtsd->bhts', qc, kc, decay)
        o = attn @ vc + jnp.einsum('bhtd,bhdv->bhtv', qc * jnp.exp(b), S)
        b_last = b[:, :, -1]
        S_new = jnp.exp(b_last)[..., None] * S + jnp.einsum('bhsd,bhsv->bhdv', kc * jnp.exp(b_last[:, :, None] - b), vc)
        return S_new, o

    S, o = lax.scan(step, s0.astype(jnp.float32), (to_chunks(q), to_chunks(k), to_chunks(log_f), to_chunks(v)))
    return o.transpose(1, 0, 3, 2, 4).reshape(B, T, H, DV), S


def moe_ffn(x, l, w_router, b_router, w_gate_up, b_gate_up, w_down, b_down):
    n, d = x.shape
    n_assign = n * TOP_K
    R = max(1, min(MOE_BLOCK_ROWS, n_assign // N_EXPERTS))
    n_blocks = -(-(n_assign + N_EXPERTS * (R - 1)) // R)
    n_rows = n_blocks * R
    logits = (x @ w_router[l] + b_router[l]).astype(jnp.float32)
    top_logits, top_idx = lax.top_k(logits, TOP_K)
    gates = jax.nn.softmax(top_logits, axis=-1)
    expert = top_idx.reshape(-1).astype(jnp.int32)
    token = jnp.arange(n_assign, dtype=jnp.int32) // TOP_K
    gate = gates.reshape(-1)
    order = jnp.argsort(expert)
    expert_s = expert[order]
    counts = jnp.zeros((N_EXPERTS,), jnp.int32).at[expert].add(1)
    padded = (counts + R - 1) // R * R
    start = jnp.cumsum(counts) - counts
    pend = jnp.cumsum(padded)
    pstart = pend - padded
    dest = pstart[expert_s] + jnp.arange(n_assign, dtype=jnp.int32) - start[expert_s]
    row_token = jnp.zeros((n_rows,), jnp.int32).at[dest].set(token[order])
    row_gate = jnp.zeros((n_rows,), jnp.float32).at[dest].set(gate[order])
    block_expert = jnp.minimum(jnp.searchsorted(pend, jnp.arange(n_blocks, dtype=jnp.int32) * R, side='right'), N_EXPERTS - 1).astype(jnp.int32)

    def expert_block(args):
        e, toks = args
        gu = x[toks] @ w_gate_up[l, e] + b_gate_up[l, e]
        g = jnp.minimum(gu[:, :D_EXPERT], SWIGLU_LIMIT)
        u = jnp.clip(gu[:, D_EXPERT:], -SWIGLU_LIMIT, SWIGLU_LIMIT)
        return ((u + 1) * g * jax.nn.sigmoid(SWIGLU_ALPHA * g)) @ w_down[l, e] + b_down[l, e]

    y_rows = lax.map(expert_block, (block_expert, row_token.reshape(n_blocks, R)))
    return jnp.zeros_like(x).at[row_token].add(y_rows.reshape(n_rows, d) * row_gate[:, None].astype(x.dtype))


def decoder_layer(x, p, attend, hg_state0, hg_lb, l, attn_norm, w_in, fox_fgate_bias, q_norm, k_norm,
                  hg_out_norm, w_proj_fox, w_proj_hg, w_out, ffn_norm, w_router, b_router, w_gate_up,
                  b_gate_up, w_down, b_down, ple_norm, w_ple_gate, w_ple_proj):
    B, T, _ = x.shape
    xn = rms_norm(x, attn_norm[l])
    fq, fk, fv, ff, hq, hf, hi, hgt, ga, gb = split_columns(xn @ w_in[l])
    q = rms_norm(fq.reshape(B, T, H_FOX, HEAD_DIM), q_norm[l])
    k = rms_norm(fk.reshape(B, T, H_FOX, HEAD_DIM), k_norm[l])
    v = fv.reshape(B, T, H_FOX, HEAD_DIM)
    fox_logf = jax.nn.log_sigmoid((ff + fox_fgate_bias[l]).astype(jnp.float32))
    o_fox = attend(q, k, v, fox_logf).reshape(B, T, FOX_WIDTH)
    forget = hg_lb + (1.0 - hg_lb) * jax.nn.sigmoid(hf.astype(jnp.float32))
    o_hg, hg_state = hgrn2_chunked(jax.nn.silu(hq).reshape(B, T, H_HG, HG_DK),
                                   (1.0 - forget).reshape(B, T, H_HG, HG_DK),
                                   jnp.log(forget).reshape(B, T, H_HG, HG_DK),
                                   hi.reshape(B, T, H_HG, HG_DV), hg_state0)
    o_hg = rms_norm(o_hg.astype(x.dtype), hg_out_norm[l]) * jax.nn.silu(hgt.reshape(B, T, H_HG, HG_DV))
    merged = (jax.nn.sigmoid(ga) * (o_fox @ w_proj_fox[l])
              + jax.nn.sigmoid(gb) * (o_hg.reshape(B, T, HG_WIDTH) @ w_proj_hg[l]))
    h = x + merged @ w_out[l]
    hn = rms_norm(h, ffn_norm[l]).reshape(B * T, D_MODEL)
    h = h + moe_ffn(hn, l, w_router, b_router, w_gate_up, b_gate_up, w_down, b_down).reshape(B, T, D_MODEL)
    ple_gate = jax.nn.sigmoid(rms_norm(h, ple_norm[l]) @ w_ple_gate[l])
    h = h + ple_gate * (p @ w_ple_proj[l])
    return h, k, v, fox_logf, hg_state


def setup_inputs(seed: int = 0) -> dict:
    key = jax.random.key(seed)
    ks = jax.random.split(key, 32)
    f32 = jnp.float32
    n_pages = PAST_LEN // PAGE_SIZE
    n_used = DEC_BATCH * n_pages
    n_phys = n_used + n_used // 4

    def nrm(k, shape, scale):
        return scale * jax.random.normal(k, shape, f32)

    return {
        'x_prompt': nrm(ks[0], (BATCH, SEQ, D_MODEL), 1.0),
        'x_sample': nrm(ks[1], (DEC_BATCH, DEC_SEQ, D_MODEL), 1.0),
        'cache_k': nrm(ks[2], (DEPTH, n_phys, PAGE_SIZE, H_FOX, HEAD_DIM), 1.0),
        'cache_v': nrm(ks[3], (DEPTH, n_phys, PAGE_SIZE, H_FOX, HEAD_DIM), 1.0),
        'cache_logf': jax.nn.log_sigmoid(FOX_FGATE_BIAS_INIT + nrm(ks[4], (DEPTH, n_phys, PAGE_SIZE, H_FOX), 0.5)),
        'state_hgrn': nrm(ks[5], (DEPTH, DEC_BATCH, H_HG, HG_DK, HG_DV), 0.5),
        'page_table': jax.random.permutation(ks[6], n_phys)[:n_used].reshape(DEC_BATCH, n_pages).astype(jnp.int32),
        'p_prompt': nrm(ks[7], (DEPTH, BATCH, SEQ, PLE_DIM), 1.0),
        'p_sample': nrm(ks[8], (DEPTH, DEC_BATCH, DEC_SEQ, PLE_DIM), 1.0),
        'attn_norm': 1.0 + nrm(ks[9], (DEPTH, D_MODEL), 0.02),
        'w_in': nrm(ks[10], (DEPTH, D_MODEL, IN_COLS), D_MODEL ** -0.5),
        'fox_fgate_bias': FOX_FGATE_BIAS_INIT + nrm(ks[11], (DEPTH, H_FOX), 0.1),
        'q_norm': 1.0 + nrm(ks[12], (DEPTH, HEAD_DIM), 0.02),
        'k_norm': 1.0 + nrm(ks[13], (DEPTH, HEAD_DIM), 0.02),
        'hg_lower_bound': nrm(ks[14], (DEPTH + 1, HG_WIDTH), 0.1),
        'hg_out_norm': 1.0 + nrm(ks[15], (DEPTH, HG_DV), 0.02),
        'w_proj_fox': nrm(ks[16], (DEPTH, FOX_WIDTH, D_MODEL), FOX_WIDTH ** -0.5),
        'w_proj_hg': nrm(ks[17], (DEPTH, HG_WIDTH, D_MODEL), HG_WIDTH ** -0.5),
        'w_out': nrm(ks[18], (DEPTH, D_MODEL, D_MODEL), D_MODEL ** -0.5),
        'ffn_norm': 1.0 + nrm(ks[19], (DEPTH, D_MODEL), 0.02),
        'w_router': nrm(ks[20], (DEPTH, D_MODEL, N_EXPERTS), D_MODEL ** -0.5),
        'b_router': nrm(ks[21], (DEPTH, N_EXPERTS), 0.01),
        'w_gate_up': nrm(ks[22], (DEPTH, N_EXPERTS, D_MODEL, 2 * D_EXPERT), D_MODEL ** -0.5),
        'b_gate_up': nrm(ks[23], (DEPTH, N_EXPERTS, 2 * D_EXPERT), 0.01),
        'w_down': nrm(ks[24], (DEPTH, N_EXPERTS, D_EXPERT, D_MODEL), D_EXPERT ** -0.5),
        'b_down': nrm(ks[25], (DEPTH, N_EXPERTS, D_MODEL), 0.01),
        'ple_norm': 1.0 + nrm(ks[26], (DEPTH, D_MODEL), 0.02),
        'w_ple_gate': nrm(ks[27], (DEPTH, D_MODEL, D_MODEL), D_MODEL ** -0.5),
        'w_ple_proj': nrm(ks[28], (DEPTH, PLE_DIM, D_MODEL), PLE_DIM ** -0.5),
    }


def reference(x_prompt, x_sample, cache_k, cache_v, cache_logf, state_hgrn, page_table, p_prompt, p_sample,
              attn_norm, w_in, fox_fgate_bias, q_norm, k_norm, hg_lower_bound, hg_out_norm, w_proj_fox,
              w_proj_hg, w_out, ffn_norm, w_router, b_router, w_gate_up, b_gate_up, w_down, b_down,
              ple_norm, w_ple_gate, w_ple_proj):
    lb_all = jnp.cumsum(jax.nn.softmax(hg_lower_bound.astype(jnp.float32), axis=0), axis=0)
    hp, hs = x_prompt, x_sample
    kp, vp, fp, sp, ksm, vsm, fsm, ssm = [], [], [], [], [], [], [], []
    for l in range(DEPTH):
        s0 = jnp.zeros((hp.shape[0], H_HG, HG_DK, HG_DV), jnp.float32)
        hp, k1, v1, f1, st1 = decoder_layer(
            hp, p_prompt[l], fox_prompt_attention, s0, lb_all[l], l, attn_norm, w_in, fox_fgate_bias, q_norm,
            k_norm, hg_out_norm, w_proj_fox, w_proj_hg, w_out, ffn_norm, w_router, b_router, w_gate_up,
            b_gate_up, w_down, b_down, ple_norm, w_ple_gate, w_ple_proj)
        attend_sample = functools.partial(fox_paged_attention, cache_k=cache_k, cache_v=cache_v,
                                          cache_logf=cache_logf, page_table=page_table, layer=l)
        hs, k2, v2, f2, st2 = decoder_layer(
            hs, p_sample[l], attend_sample, state_hgrn[l], lb_all[l], l, attn_norm, w_in, fox_fgate_bias, q_norm,
            k_norm, hg_out_norm, w_proj_fox, w_proj_hg, w_out, ffn_norm, w_router, b_router, w_gate_up,
            b_gate_up, w_down, b_down, ple_norm, w_ple_gate, w_ple_proj)
        kp.append(k1); vp.append(v1); fp.append(f1); sp.append(st1)
        ksm.append(k2); vsm.append(v2); fsm.append(f2); ssm.append(st2)
    y_prompt, y_sample = hp, hs
    k_prompt, v_prompt, logf_prompt, hg_state_prompt = jnp.stack(kp), jnp.stack(vp), jnp.stack(fp), jnp.stack(sp)
    k_sample, v_sample, logf_sample, hg_state_sample = jnp.stack(ksm), jnp.stack(vsm), jnp.stack(fsm), jnp.stack(ssm)
    return (y_prompt, y_sample, k_prompt, v_prompt, logf_prompt, hg_state_prompt,
            k_sample, v_sample, logf_sample, hg_state_sample)
```

```python
import functools

import jax
import jax.numpy as jnp
from jax import lax
from jax.experimental import pallas as pl
from jax.experimental.pallas import tpu as pltpu

F32 = jnp.float32
BF16 = jnp.bfloat16

TOP_K = 4
SWIGLU_LIMIT = 7.0
SWIGLU_ALPHA = 1.702
RMS_EPS = 1e-6
LANES = 128
MASK_VALUE = -1e30
VMEM_LIMIT = 56 * 1024 * 1024


def _cparams(sem):
    return pltpu.CompilerParams(dimension_semantics=sem, vmem_limit_bytes=VMEM_LIMIT)


def _divisor(n, cap, mult):
    best = None
    for d in range(mult, min(n, cap) + 1, mult):
        if n % d == 0:
            best = d
    assert best is not None, (n, cap, mult)
    return best


def _dot(a, b):
    return jnp.dot(a, b, preferred_element_type=F32)


def _dot_nt(a, b):
    return lax.dot_general(a, b, (((1,), (1,)), ((), ())), preferred_element_type=F32)


def _dot_tn(a, b):
    return lax.dot_general(a, b, (((0,), (0,)), ((), ())), preferred_element_type=F32)


def _split3(x):
    x1 = x.astype(BF16)
    r1 = x - x1.astype(F32)
    x2 = r1.astype(BF16)
    x3 = (r1 - x2.astype(F32)).astype(BF16)
    return x1, x2, x3


def _sel_dot(sel, x):
    x1, x2, x3 = _split3(x)
    return _dot(sel, x1) + _dot(sel, x2) + _dot(sel, x3)


def _sigmoid(x):
    return 1.0 / (1.0 + jnp.exp(-x))


def _iota(shape, dim):
    return lax.broadcasted_iota(jnp.int32, shape, dim)


def _inproj_kernel(x_ref, g_ref, w_ref, o_ref, xn_ref):
    @pl.when(pl.program_id(1) == 0)
    def _():
        x = x_ref[...]
        ms = jnp.mean(x * x, axis=-1, keepdims=True)
        xn_ref[...] = (x * lax.rsqrt(ms + RMS_EPS) * g_ref[...]).astype(BF16)

    o_ref[...] = _dot(xn_ref[...], w_ref[...])


def _inproj(x, gain, w, tm, tn):
    m, d = x.shape
    n = w.shape[1]
    return pl.pallas_call(
        _inproj_kernel,
        grid=(m // tm, n // tn),
        in_specs=[pl.BlockSpec((tm, d), lambda i, j: (i, 0)),
                  pl.BlockSpec((1, d), lambda i, j: (0, 0)),
                  pl.BlockSpec((d, tn), lambda i, j: (0, j))],
        out_specs=pl.BlockSpec((tm, tn), lambda i, j: (i, j)),
        out_shape=jax.ShapeDtypeStruct((m, n), F32),
        scratch_shapes=[pltpu.VMEM((tm, d), BF16)],
        compiler_params=_cparams(("parallel", "arbitrary")),
        name="inproj",
    )(x, gain, w)


def _prep_kernel(fq_ref, fk_ref, ff_ref, qg_ref, kg_ref, fb_ref, q_ref, k_ref, lf_ref, *, nh):
    qg = qg_ref[...]
    kg = kg_ref[...]
    for h in range(nh):
        sl = slice(h * LANES, (h + 1) * LANES)
        for src, gain, dst in ((fq_ref, qg, q_ref), (fk_ref, kg, k_ref)):
            xh = src[:, sl]
            ms = jnp.mean(xh * xh, axis=-1, keepdims=True)
            dst[:, sl] = xh * lax.rsqrt(ms + RMS_EPS) * gain
    z = ff_ref[...] + fb_ref[...]
    lf_ref[...] = jnp.minimum(z, 0.0) - jnp.log1p(jnp.exp(-jnp.abs(z)))


def _prep(proj, q_gain, k_gain, fbias, tm, fw, ff_blk, ffw):
    m = proj.shape[0]
    nh = fw // LANES
    return pl.pallas_call(
        functools.partial(_prep_kernel, nh=nh),
        grid=(m // tm,),
        in_specs=[pl.BlockSpec((tm, fw), lambda i: (i, 0)),
                  pl.BlockSpec((tm, fw), lambda i: (i, 1)),
                  pl.BlockSpec((tm, ffw), lambda i: (i, ff_blk)),
                  pl.BlockSpec((1, LANES), lambda i: (0, 0)),
                  pl.BlockSpec((1, LANES), lambda i: (0, 0)),
                  pl.BlockSpec((1, ffw), lambda i: (0, 0))],
        out_specs=[pl.BlockSpec((tm, fw), lambda i: (i, 0)),
                   pl.BlockSpec((tm, fw), lambda i: (i, 0)),
                   pl.BlockSpec((tm, ffw), lambda i: (i, 0))],
        out_shape=[jax.ShapeDtypeStruct((m, fw), F32),
                   jax.ShapeDtypeStruct((m, fw), F32),
                   jax.ShapeDtypeStruct((m, ffw), F32)],
        compiler_params=_cparams(("parallel",)),
        name="fox_prep",
    )(proj, proj, proj, q_gain, k_gain, fbias)


def _cumsum_kernel(x_ref, o_ref, *, t, c):
    tri = (_iota((c, c), 0) >= _iota((c, c), 1)).astype(BF16)
    carry = jnp.zeros((1, x_ref.shape[1]), F32)
    for n in range(t // c):
        blk = _sel_dot(tri, x_ref[n * c:(n + 1) * c, :]) + carry
        o_ref[n * c:(n + 1) * c, :] = blk
        carry = blk[c - 1:c, :]


def _seq_cumsum(x, nb, t):
    w = x.shape[1]
    return pl.pallas_call(
        functools.partial(_cumsum_kernel, t=t, c=LANES),
        grid=(nb,),
        in_specs=[pl.BlockSpec((t, w), lambda b: (b, 0))],
        out_specs=pl.BlockSpec((t, w), lambda b: (b, 0)),
        out_shape=jax.ShapeDtypeStruct((nb * t, w), F32),
        compiler_params=_cparams(("parallel",)),
        name="logf_cumsum",
    )(x)


def _fox_attn_kernel(q_ref, k_ref, v_ref, cc_ref, cr_ref, o_ref, *, tq, scale):
    h = pl.program_id(1)
    i = pl.program_id(2)
    q = q_ref[...].astype(BF16)
    lane = _iota(cc_ref.shape, 1)
    cq = jnp.sum(jnp.where(lane == h, cc_ref[...], 0.0), axis=-1, keepdims=True)
    qpos = i * tq + _iota((tq, tq), 0)
    kofs = _iota((tq, tq), 1)

    def body(j, carry):
        m, l, acc = carry
        r0 = pl.multiple_of(j * tq, tq)
        ks = k_ref[pl.ds(r0, tq), :].astype(BF16)
        vs = v_ref[pl.ds(r0, tq), :].astype(BF16)
        ck = cr_ref[:, pl.ds(r0, tq)]
        s = _dot_nt(q, ks) * scale + (cq - ck)
        s = jnp.where(qpos >= j * tq + kofs, s, MASK_VALUE)
        m_new = jnp.maximum(m, jnp.max(s, axis=-1, keepdims=True))
        alpha = jnp.exp(m - m_new)
        p = jnp.exp(s - m_new)
        l = alpha * l + jnp.sum(p, axis=-1, keepdims=True)
        acc = alpha * acc + _dot(p.astype(BF16), vs)
        return m_new, l, acc

    init = (jnp.full((tq, 1), MASK_VALUE, F32), jnp.zeros((tq, 1), F32), jnp.zeros((tq, LANES), F32))
    _, l, acc = lax.fori_loop(0, i + 1, body, init)
    o_ref[...] = (acc / l).astype(o_ref.dtype)


def _fox_attn(qn, kn, proj, c, c_t, nb, t, nh, v_blk, tq):
    nq = t // tq
    cw = c.shape[1]
    return pl.pallas_call(
        functools.partial(_fox_attn_kernel, tq=tq, scale=LANES ** -0.5),
        grid=(nb, nh, nq),
        in_specs=[pl.BlockSpec((tq, LANES), lambda b, h, i: (b * nq + i, h)),
                  pl.BlockSpec((t, LANES), lambda b, h, i: (b, h)),
                  pl.BlockSpec((t, LANES), lambda b, h, i: (b, v_blk + h)),
                  pl.BlockSpec((tq, cw), lambda b, h, i: (b * nq + i, 0)),
                  pl.BlockSpec((None, 1, t), lambda b, h, i: (b * nh + h, 0, 0))],
        out_specs=pl.BlockSpec((tq, LANES), lambda b, h, i: (b * nq + i, h)),
        out_shape=jax.ShapeDtypeStruct((nb * t, nh * LANES), BF16),
        compiler_params=_cparams(("parallel", "parallel", "arbitrary")),
        name="fox_attn",
    )(qn, kn, proj, c, c_t)


def _page_bias_kernel(pt_ref, *refs, n_pages, page):
    lf_refs = refs[:n_pages]
    new_ref = refs[n_pages]
    o_ref = refs[n_pages + 1]
    strict_upper = (_iota((page, page), 0) < _iota((page, page), 1)).astype(BF16)
    carry = new_ref[...]
    for p in range(n_pages - 1, -1, -1):
        x = lf_refs[p][...]
        suf = _sel_dot(strict_upper, x) + carry
        o_ref[p * page:(p + 1) * page, :] = suf
        carry = suf[0:1, :] + x[0:1, :]


def _page_bias(page_table, cache_logf, lf_new):
    nbs, n_pages = page_table.shape
    _, _, page, nh = cache_logf.shape
    in_specs = [pl.BlockSpec((None, None, page, nh),
                             functools.partial(lambda b, pt, p: (0, pt[b * n_pages + p], 0, 0), p=p))
                for p in range(n_pages)]
    in_specs.append(pl.BlockSpec((None, 1, nh), lambda b, pt: (b, 0, 0)))
    return pl.pallas_call(
        functools.partial(_page_bias_kernel, n_pages=n_pages, page=page),
        grid_spec=pltpu.PrefetchScalarGridSpec(
            num_scalar_prefetch=1,
            grid=(nbs,),
            in_specs=in_specs,
            out_specs=pl.BlockSpec((None, n_pages * page, nh), lambda b, pt: (b, 0, 0))),
        out_shape=jax.ShapeDtypeStruct((nbs, n_pages * page, nh), F32),
        compiler_params=_cparams(("arbitrary",)),
        name="page_bias",
    )(page_table.reshape(-1), *([cache_logf] * n_pages), lf_new)


def _paged_attn_kernel(pt_ref, q_ref, kn_ref, vn_ref, k_ref, v_ref, b_ref, o_ref, m_ref, l_ref, acc_ref,
                       *, scale, page, nh):
    p = pl.program_id(1)
    q = q_ref[...]

    @pl.when(p == 0)
    def _():
        s_new = jnp.sum(q * kn_ref[...], axis=-1, keepdims=True) * scale
        m_ref[...] = jnp.broadcast_to(s_new, m_ref.shape)
        l_ref[...] = jnp.ones(l_ref.shape, F32)
        acc_ref[...] = vn_ref[...]

    rows = page * nh
    groups = rows // LANES
    kq = (k_ref[...] * q[None]).reshape(groups, LANES, LANES) * scale
    eye = _iota((LANES, LANES), 0) == _iota((LANES, LANES), 1)
    z = kq + jnp.where(eye[None], b_ref[...][:, None, :], 0.0)
    z = z.reshape(rows, LANES)
    hi = z.astype(BF16)
    lo = (z - hi.astype(F32)).astype(BF16)
    ones = jnp.ones((2 * LANES, LANES), BF16)
    s = _dot(jnp.concatenate([hi, lo], axis=1), ones).reshape(page, nh, LANES)
    m_old = m_ref[...]
    m_new = jnp.maximum(m_old, jnp.max(s, axis=0))
    alpha = jnp.exp(m_old - m_new)
    pr = jnp.exp(s - m_new[None])
    l_ref[...] = alpha * l_ref[...] + jnp.sum(pr, axis=0)
    acc_ref[...] = alpha * acc_ref[...] + jnp.sum(pr * v_ref[...], axis=0)
    m_ref[...] = m_new

    @pl.when(p == pl.num_programs(1) - 1)
    def _():
        o_ref[...] = acc_ref[...] / l_ref[...]


def _paged_attn(page_table, q, k_new, v_new, cache_k, cache_v, bias):
    nbs, n_pages = page_table.shape
    _, _, page, nh, dh = cache_k.shape
    groups = page * nh // LANES
    tok = pl.BlockSpec((None, nh, dh), lambda b, p, pt: (b, 0, 0))
    pg = pl.BlockSpec((None, None, page, nh, dh), lambda b, p, pt: (0, pt[b * n_pages + p], 0, 0, 0))
    return pl.pallas_call(
        functools.partial(_paged_attn_kernel, scale=dh ** -0.5, page=page, nh=nh),
        grid_spec=pltpu.PrefetchScalarGridSpec(
            num_scalar_prefetch=1,
            grid=(nbs, n_pages),
            in_specs=[tok, tok, tok, pg, pg,
                      pl.BlockSpec((None, None, groups, LANES), lambda b, p, pt: (b, p, 0, 0))],
            out_specs=tok,
            scratch_shapes=[pltpu.VMEM((nh, dh), F32)] * 3),
        out_shape=jax.ShapeDtypeStruct((nbs, nh, dh), F32),
        compiler_params=_cparams(("parallel", "arbitrary")),
        name="paged_attn",
    )(page_table.reshape(-1), q, k_new, v_new, cache_k, cache_v, bias)


def _hgrn_levels(c):
    levels = []
    m = 1
    while m < c:
        levels.append(m)
        m *= 2
    return levels


def _hgrn_prompt_kernel(hq_ref, hf_ref, hi_ref, hg_ref, lb_ref, gain_ref, o_ref, st_ref, *, t, c):
    levels = _hgrn_levels(c)
    row = _iota((c, c), 0)
    col = _iota((c, c), 1)
    tri = (row >= col).astype(BF16)
    eye = row == col
    sels, masks = [], []
    for m in levels:
        boundary = (row // (2 * m)) * (2 * m) + (m - 1)
        sels.append((col == boundary).astype(BF16))
        same = (row // (2 * m)) == (col // (2 * m))
        masks.append(same & ((row % (2 * m)) >= m) & ((col % (2 * m)) < m))
    sel_all = jnp.concatenate(sels, axis=0)
    lb = lb_ref[...]
    gain = gain_ref[...]

    def chunk(n, s_t):
        r0 = pl.multiple_of(n * c, c)
        hq = hq_ref[pl.ds(r0, c), :]
        fg = lb + (1.0 - lb) * _sigmoid(hf_ref[pl.ds(r0, c), :])
        q = hq * _sigmoid(hq)
        k = 1.0 - fg
        g = jnp.log(fg)
        v = hi_ref[pl.ds(r0, c), :].astype(BF16)
        b = _sel_dot(tri, g)
        refs = _sel_dot(sel_all, b)
        a = jnp.where(eye, _dot_nt(q.astype(BF16), k.astype(BF16)), 0.0)
        for li in range(len(levels)):
            rl = refs[li * c:(li + 1) * c, :]
            ql = q * jnp.exp(jnp.minimum(b - rl, 0.0))
            kl = k * jnp.exp(jnp.minimum(rl - b, 0.0))
            a = a + jnp.where(masks[li], _dot_nt(ql.astype(BF16), kl.astype(BF16)), 0.0)
        o = _dot(a.astype(BF16), v) + _dot_nt((q * jnp.exp(b)).astype(BF16), s_t.astype(BF16))
        b_last = b[c - 1:c, :]
        kd = k * jnp.exp(b_last - b)
        s_t = s_t * jnp.exp(b_last) + _dot_tn(v, kd.astype(BF16))
        ms = jnp.mean(o * o, axis=-1, keepdims=True)
        hg = hg_ref[pl.ds(r0, c), :]
        o_ref[pl.ds(r0, c), :] = (o * lax.rsqrt(ms + RMS_EPS) * gain * (hg * _sigmoid(hg))).astype(o_ref.dtype)
        return s_t

    s_t = lax.fori_loop(0, t // c, chunk, jnp.zeros((LANES, LANES), F32))
    ident = eye.astype(BF16) if c == LANES else (_iota((LANES, LANES), 0) == _iota((LANES, LANES), 1)).astype(BF16)
    s1, s2, s3 = _split3(s_t)
    st_ref[...] = _dot_nt(ident, s1) + _dot_nt(ident, s2) + _dot_nt(ident, s3)


def _hgrn_prompt(proj, lb, gain, nb, t, nh, blk_q, blk_f, blk_i, blk_g, c):
    def col(blk):
        return pl.BlockSpec((t, LANES), lambda b, h: (b, blk + h))
    return pl.pallas_call(
        functools.partial(_hgrn_prompt_kernel, t=t, c=c),
        grid=(nb, nh),
        in_specs=[col(blk_q), col(blk_f), col(blk_i), col(blk_g),
                  pl.BlockSpec((1, LANES), lambda b, h: (0, h)),
                  pl.BlockSpec((1, LANES), lambda b, h: (0, 0))],
        out_specs=[pl.BlockSpec((t, LANES), lambda b, h: (b, h)),
                   pl.BlockSpec((None, None, LANES, LANES), lambda b, h: (b, h, 0, 0))],
        out_shape=[jax.ShapeDtypeStruct((nb * t, nh * LANES), BF16),
                   jax.ShapeDtypeStruct((nb, nh, LANES, LANES), F32)],
        compiler_params=_cparams(("parallel", "parallel")),
        name="hgrn_prompt",
    )(proj, proj, proj, proj, lb, gain)


def _hgrn_sample_kernel(hq_ref, hf_ref, hi_ref, hg_ref, lb_ref, gain_ref, s_ref, o_ref, so_ref, *, group):
    eye = _iota((LANES, LANES), 0) == _iota((LANES, LANES), 1)

    def column(rowvec):
        return jnp.sum(jnp.where(eye, jnp.broadcast_to(rowvec, (LANES, LANES)), 0.0), axis=-1, keepdims=True)

    lb = lb_ref[...]
    gain = gain_ref[...]
    for s in range(group):
        hq = hq_ref[s:s + 1, :]
        fg = lb + (1.0 - lb) * _sigmoid(hf_ref[s:s + 1, :])
        q = hq * _sigmoid(hq)
        s_new = column(fg) * s_ref[s] + column(1.0 - fg) * hi_ref[s:s + 1, :]
        so_ref[s] = s_new
        o = jnp.sum(column(q) * s_new, axis=0, keepdims=True)
        ms = jnp.mean(o * o, axis=-1, keepdims=True)
        hg = hg_ref[s:s + 1, :]
        o_ref[s:s + 1, :] = o * lax.rsqrt(ms + RMS_EPS) * gain * (hg * _sigmoid(hg))


def _hgrn_sample(proj, state, lb, gain, row_blk0, nbs, nh, blk_q, blk_f, blk_i, blk_g, group):
    def col(blk):
        return pl.BlockSpec((group, LANES), lambda h, i: (row_blk0 + i, blk + h))
    st = pl.BlockSpec((None, group, None, LANES, LANES), lambda h, i: (0, i, h, 0, 0))
    return pl.pallas_call(
        functools.partial(_hgrn_sample_kernel, group=group),
        grid=(nh, nbs // group),
        in_specs=[col(blk_q), col(blk_f), col(blk_i), col(blk_g),
                  pl.BlockSpec((1, LANES), lambda h, i: (0, h)),
                  pl.BlockSpec((1, LANES), lambda h, i: (0, 0)),
                  st],
        out_specs=[pl.BlockSpec((group, LANES), lambda h, i: (i, h)), st],
        out_shape=[jax.ShapeDtypeStruct((nbs, nh * LANES), F32),
                   jax.ShapeDtypeStruct((1,) + state.shape[1:], F32)],
        compiler_params=_cparams(("parallel", "parallel")),
        name="hgrn_sample",
    )(proj, proj, proj, proj, lb, gain, state)


def _merge_kernel(of_ref, oh_ref, wf_ref, wh_ref, ga_ref, gb_ref, o_ref):
    a = _dot(of_ref[...], wf_ref[...])
    b = _dot(oh_ref[...], wh_ref[...])
    o_ref[...] = (_sigmoid(ga_ref[...]) * a + _sigmoid(gb_ref[...]) * b).astype(o_ref.dtype)


def _merge(o_fox, o_hg, wf, wh, proj, blk_ga, blk_gb, tm, tn):
    m, kf = o_fox.shape
    n = wf.shape[1]
    return pl.pallas_call(
        _merge_kernel,
        grid=(m // tm, n // tn),
        in_specs=[pl.BlockSpec((tm, kf), lambda i, j: (i, 0)),
                  pl.BlockSpec((tm, kf), lambda i, j: (i, 0)),
                  pl.BlockSpec((kf, tn), lambda i, j: (0, j)),
                  pl.BlockSpec((kf, tn), lambda i, j: (0, j)),
                  pl.BlockSpec((tm, tn), lambda i, j: (i, blk_ga + j)),
                  pl.BlockSpec((tm, tn), lambda i, j: (i, blk_gb + j))],
        out_specs=pl.BlockSpec((tm, tn), lambda i, j: (i, j)),
        out_shape=jax.ShapeDtypeStruct((m, n), BF16),
        compiler_params=_cparams(("parallel", "arbitrary")),
        name="branch_merge",
    )(o_fox, o_hg, wf, wh, proj, proj)


def _outproj_kernel(mg_ref, w_ref, x_ref, g_ref, wr_ref, br_ref, h_ref, hn_ref, lg_ref, hrow_ref, *, tn):
    j = pl.program_id(1)
    hblk = x_ref[...] + _dot(mg_ref[...], w_ref[...])
    h_ref[...] = hblk
    c0 = pl.multiple_of(j * tn, tn)
    hrow_ref[:, pl.ds(c0, tn)] = hblk

    @pl.when(j == pl.num_programs(1) - 1)
    def _():
        h = hrow_ref[...]
        ms = jnp.mean(h * h, axis=-1, keepdims=True)
        hn = h * lax.rsqrt(ms + RMS_EPS) * g_ref[...]
        hn_ref[...] = hn
        lg_ref[...] = _dot(hn.astype(BF16), wr_ref[...]) + br_ref[...]


def _outproj(merged, w_out, x, gain, w_router, b_router, tm, tn):
    m, d = x.shape
    ne = w_router.shape[1]
    return pl.pallas_call(
        functools.partial(_outproj_kernel, tn=tn),
        grid=(m // tm, d // tn),
        in_specs=[pl.BlockSpec((tm, d), lambda i, j: (i, 0)),
                  pl.BlockSpec((d, tn), lambda i, j: (0, j)),
                  pl.BlockSpec((tm, tn), lambda i, j: (i, j)),
                  pl.BlockSpec((1, d), lambda i, j: (0, 0)),
                  pl.BlockSpec((d, ne), lambda i, j: (0, 0)),
                  pl.BlockSpec((1, ne), lambda i, j: (0, 0))],
        out_specs=[pl.BlockSpec((tm, tn), lambda i, j: (i, j)),
                   pl.BlockSpec((tm, d), lambda i, j: (i, 0)),
                   pl.BlockSpec((tm, ne), lambda i, j: (i, 0))],
        out_shape=[jax.ShapeDtypeStruct((m, d), F32),
                   jax.ShapeDtypeStruct((m, d), F32),
                   jax.ShapeDtypeStruct((m, ne), F32)],
        scratch_shapes=[pltpu.VMEM((tm, d), F32)],
        compiler_params=_cparams(("parallel", "arbitrary")),
        name="out_proj",
    )(merged, w_out, x, gain, w_router, b_router)


def _gather_kernel(tok_ref, x_hbm, o_ref, buf, sem, *, rows):
    i = pl.program_id(0)

    def copy(r):
        t = tok_ref[i * rows + r]
        return pltpu.make_async_copy(x_hbm.at[pl.ds(t, 1), :], buf.at[pl.ds(r, 1), :], sem)

    def start(r, carry):
        copy(r).start()
        return carry

    def wait(r, carry):
        copy(r).wait()
        return carry

    lax.fori_loop(0, rows, start, 0)
    lax.fori_loop(0, rows, wait, 0)
    o_ref[...] = buf[...].astype(o_ref.dtype)


def _gather_rows(row_token, x, rows):
    n_rows = row_token.shape[0]
    d = x.shape[1]
    return pl.pallas_call(
        functools.partial(_gather_kernel, rows=rows),
        grid_spec=pltpu.PrefetchScalarGridSpec(
            num_scalar_prefetch=1,
            grid=(n_rows // rows,),
            in_specs=[pl.BlockSpec(memory_space=pl.ANY)],
            out_specs=pl.BlockSpec((rows, d), lambda i, tok: (i, 0)),
            scratch_shapes=[pltpu.VMEM((rows, d), x.dtype), pltpu.SemaphoreType.DMA(())]),
        out_shape=jax.ShapeDtypeStruct((n_rows, d), BF16),
        compiler_params=_cparams(("arbitrary",)),
        name="moe_gather",
    )(row_token, x)


def _moe_up_kernel(be_ref, ne_ref, va_ref, x_ref, wg_ref, wu_ref, bg_ref, bu_ref, o_ref, wgb, wub):
    i = pl.program_id(1)

    @pl.when(ne_ref[i] == 1)
    def _():
        wgb[...] = wg_ref[...].astype(BF16)
        wub[...] = wu_ref[...].astype(BF16)

    @pl.when(va_ref[i] == 1)
    def _():
        x = x_ref[...]
        g = jnp.minimum(_dot(x, wgb[...]) + bg_ref[...], SWIGLU_LIMIT)
        u = jnp.clip(_dot(x, wub[...]) + bu_ref[...], -SWIGLU_LIMIT, SWIGLU_LIMIT)
        o_ref[...] = ((u + 1.0) * g * _sigmoid(SWIGLU_ALPHA * g)).astype(o_ref.dtype)

    @pl.when(va_ref[i] == 0)
    def _():
        o_ref[...] = jnp.zeros(o_ref.shape, o_ref.dtype)


def _moe_up(block_expert, new_expert, valid, xs, w_gate_up, b_gate_up, rows, tf):
    n_rows, d = xs.shape
    f = w_gate_up.shape[2] // 2
    nf = f // tf
    return pl.pallas_call(
        _moe_up_kernel,
        grid_spec=pltpu.PrefetchScalarGridSpec(
            num_scalar_prefetch=3,
            grid=(nf, n_rows // rows),
            in_specs=[pl.BlockSpec((rows, d), lambda j, i, be, ne, va: (i, 0)),
                      pl.BlockSpec((None, d, tf), lambda j, i, be, ne, va: (be[i], 0, j)),
                      pl.BlockSpec((None, d, tf), lambda j, i, be, ne, va: (be[i], 0, nf + j)),
                      pl.BlockSpec((None, 1, tf), lambda j, i, be, ne, va: (be[i], 0, j)),
                      pl.BlockSpec((None, 1, tf), lambda j, i, be, ne, va: (be[i], 0, nf + j))],
            out_specs=pl.BlockSpec((rows, tf), lambda j, i, be, ne, va: (i, j)),
            scratch_shapes=[pltpu.VMEM((d, tf), BF16), pltpu.VMEM((d, tf), BF16)]),
        out_shape=jax.ShapeDtypeStruct((n_rows, f), BF16),
        compiler_params=_cparams(("arbitrary", "arbitrary")),
        name="moe_up",
    )(block_expert, new_expert, valid, xs, w_gate_up, w_gate_up, b_gate_up, b_gate_up)


def _moe_down_kernel(be_ref, ne_ref, va_ref, a_ref, w_ref, b_ref, g_ref, o_ref, wb):
    i = pl.program_id(1)

    @pl.when(ne_ref[i] == 1)
    def _():
        wb[...] = w_ref[...].astype(BF16)

    @pl.when(va_ref[i] == 1)
    def _():
        o_ref[...] = (_dot(a_ref[...], wb[...]) + b_ref[...]) * g_ref[...]

    @pl.when(va_ref[i] == 0)
    def _():
        o_ref[...] = jnp.zeros(o_ref.shape, o_ref.dtype)


def _moe_down(block_expert, new_expert, valid, act, w_down, b_down, row_gate, rows, tn):
    n_rows, f = act.shape
    d = w_down.shape[2]
    return pl.pallas_call(
        _moe_down_kernel,
        grid_spec=pltpu.PrefetchScalarGridSpec(
            num_scalar_prefetch=3,
            grid=(d // tn, n_rows // rows),
            in_specs=[pl.BlockSpec((rows, f), lambda j, i, be, ne, va: (i, 0)),
                      pl.BlockSpec((None, f, tn), lambda j, i, be, ne, va: (be[i], 0, j)),
                      pl.BlockSpec((None, 1, tn), lambda j, i, be, ne, va: (be[i], 0, j)),
                      pl.BlockSpec((rows, 1), lambda j, i, be, ne, va: (i, 0))],
            out_specs=pl.BlockSpec((rows, tn), lambda j, i, be, ne, va: (i, j)),
            scratch_shapes=[pltpu.VMEM((f, tn), BF16)]),
        out_shape=jax.ShapeDtypeStruct((n_rows, d), F32),
        compiler_params=_cparams(("arbitrary", "arbitrary")),
        name="moe_down",
    )(block_expert, new_expert, valid, act, w_down, b_down, row_gate)


def _combine_kernel(pos_ref, h_ref, g_ref, y_hbm, h2_ref, hn_ref, buf, sem, *, tm, top_k):
    i = pl.program_id(0)
    n = tm * top_k

    def copy(a):
        src = pos_ref[i * n + a]
        return pltpu.make_async_copy(y_hbm.at[pl.ds(src, 1), :], buf.at[a % top_k, pl.ds(a // top_k, 1), :], sem)

    def start(a, carry):
        copy(a).start()
        return carry

    def wait(a, carry):
        copy(a).wait()
        return carry

    lax.fori_loop(0, n, start, 0)
    lax.fori_loop(0, n, wait, 0)
    h = h_ref[...]
    for k in range(top_k):
        h = h + buf[k]
    h2_ref[...] = h
    ms = jnp.mean(h * h, axis=-1, keepdims=True)
    hn_ref[...] = (h * lax.rsqrt(ms + RMS_EPS) * g_ref[...]).astype(hn_ref.dtype)


def _combine(pos, h, gain, y_rows, tm):
    m, d = h.shape
    return pl.pallas_call(
        functools.partial(_combine_kernel, tm=tm, top_k=TOP_K),
        grid_spec=pltpu.PrefetchScalarGridSpec(
            num_scalar_prefetch=1,
            grid=(m // tm,),
            in_specs=[pl.BlockSpec((tm, d), lambda i, pos: (i, 0)),
                      pl.BlockSpec((1, d), lambda i, pos: (0, 0)),
                      pl.BlockSpec(memory_space=pl.ANY)],
            out_specs=[pl.BlockSpec((tm, d), lambda i, pos: (i, 0)),
                       pl.BlockSpec((tm, d), lambda i, pos: (i, 0))],
            scratch_shapes=[pltpu.VMEM((TOP_K, tm, d), F32), pltpu.SemaphoreType.DMA(())]),
        out_shape=[jax.ShapeDtypeStruct((m, d), F32), jax.ShapeDtypeStruct((m, d), BF16)],
        compiler_params=_cparams(("arbitrary",)),
        name="moe_combine",
    )(pos, h, gain, y_rows)


def _ple_kernel(hn_ref, wg_ref, p_ref, wp_ref, h_ref, o_ref):
    gate = _sigmoid(_dot(hn_ref[...], wg_ref[...]))
    o_ref[...] = h_ref[...] + gate * _dot(p_ref[...], wp_ref[...])


def _ple(hn, w_gate, p, w_proj, h, tm, tn):
    m, d = h.shape
    pd = p.shape[1]
    return pl.pallas_call(
        _ple_kernel,
        grid=(m // tm, d // tn),
        in_specs=[pl.BlockSpec((tm, d), lambda i, j: (i, 0)),
                  pl.BlockSpec((d, tn), lambda i, j: (0, j)),
                  pl.BlockSpec((tm, pd), lambda i, j: (i, 0)),
                  pl.BlockSpec((pd, tn), lambda i, j: (0, j)),
                  pl.BlockSpec((tm, tn), lambda i, j: (i, j))],
        out_specs=pl.BlockSpec((tm, tn), lambda i, j: (i, j)),
        out_shape=jax.ShapeDtypeStruct((m, d), F32),
        compiler_params=_cparams(("parallel", "arbitrary")),
        name="ple",
    )(hn, w_gate, p, w_proj, h)


def _route(logits, n_experts, rows):
    n = logits.shape[0]
    n_assign = n * TOP_K
    n_blocks = (n_assign + n_experts * (rows - 1) + rows - 1) // rows
    n_rows = n_blocks * rows
    top_logits, top_idx = lax.top_k(logits, TOP_K)
    gate = jax.nn.softmax(top_logits, axis=-1).reshape(-1)
    expert = top_idx.reshape(-1).astype(jnp.int32)
    token = jnp.arange(n_assign, dtype=jnp.int32) // TOP_K
    onehot = (expert[:, None] == jnp.arange(n_experts, dtype=jnp.int32)[None, :]).astype(jnp.int32)
    csum = jnp.cumsum(onehot, axis=0)
    rank = jnp.take_along_axis(csum, expert[:, None], axis=1)[:, 0] - 1
    counts = csum[-1]
    padded = (counts + rows - 1) // rows * rows
    pend = jnp.cumsum(padded)
    pos = (pend - padded)[expert] + rank
    row_token = jnp.zeros((n_rows,), jnp.int32).at[pos].set(token)
    row_gate = jnp.zeros((n_rows,), F32).at[pos].set(gate)
    blk_start = jnp.arange(n_blocks, dtype=jnp.int32) * rows
    block_expert = jnp.minimum(jnp.searchsorted(pend, blk_start, side='right'), n_experts - 1).astype(jnp.int32)
    valid = (blk_start < pend[-1]).astype(jnp.int32)
    last_valid = block_expert[jnp.maximum(pend[-1] // rows - 1, 0)]
    block_expert = jnp.where(valid == 1, block_expert, last_valid)
    prev = jnp.concatenate([jnp.full((1,), -1, jnp.int32), block_expert[:-1]])
    new_expert = (block_expert != prev).astype(jnp.int32)
    return pos.astype(jnp.int32), row_token, row_gate, block_expert, new_expert, valid


def kernel(x_prompt, x_sample, cache_k, cache_v, cache_logf, state_hgrn, page_table, p_prompt, p_sample,
           attn_norm, w_in, fox_fgate_bias, q_norm, k_norm, hg_lower_bound, hg_out_norm, w_proj_fox,
           w_proj_hg, w_out, ffn_norm, w_router, b_router, w_gate_up, b_gate_up, w_down, b_down,
           ple_norm, w_ple_gate, w_ple_proj):
    nb, t, d = x_prompt.shape
    nbs = x_sample.shape[0]
    assert x_sample.shape[1] == 1 and cache_k.shape[0] == 1
    _, _, page, nh, dh = cache_k.shape
    assert dh == LANES
    fw = nh * dh
    nhg = state_hgrn.shape[2]
    hw = nhg * LANES
    assert state_hgrn.shape[3] == LANES and state_hgrn.shape[4] == LANES and hw == fw
    n_experts = w_router.shape[2]
    mp = nb * t
    m = mp + nbs

    tm = _divisor(m, 704, 16)
    tn = _divisor(fw, 512, LANES)
    tq = _divisor(t, 256, LANES)
    chunk = LANES
    group = 8
    moe_rows = 256 if m * TOP_K // n_experts >= 512 else 16
    tm_out = _divisor(m, 320, 16)
    tf = _divisor(d, 512, LANES)
    tdn = _divisor(d, 1024, LANES)
    assert nbs % group == 0 and mp % group == 0 and t % chunk == 0

    x = jnp.concatenate([x_prompt.reshape(mp, d), x_sample.reshape(nbs, d)], axis=0)
    p_all = jnp.concatenate([p_prompt[0].reshape(mp, -1), p_sample[0].reshape(nbs, -1)], axis=0).astype(BF16)
    w = w_in[0]
    ff0 = 3 * fw
    w_al = jnp.concatenate([w[:, :ff0], w[:, ff0 + nh:],
                            jnp.pad(w[:, ff0:ff0 + nh], ((0, 0), (0, tn - nh)))], axis=1).astype(BF16)
    blk = lambda col: col // LANES
    c_hq, c_hf, c_hi, c_hg = ff0, ff0 + hw, ff0 + 2 * hw, ff0 + 3 * hw
    c_ga = ff0 + 4 * hw
    c_gb = c_ga + d
    c_ff = c_gb + d
    lb = jax.nn.softmax(hg_lower_bound.astype(F32), axis=0)[0:1]
    fbias = jnp.pad(fox_fgate_bias, ((0, 0), (0, LANES - nh)))

    proj = _inproj(x, attn_norm, w_al, tm, tn)
    qn, kn, logf = _prep(proj, q_norm, k_norm, fbias, _divisor(m, 512, 8), fw, blk(c_ff), LANES)

    c = _seq_cumsum(logf[:mp], nb, t)
    c_t = c[:, :nh].reshape(nb, t, nh).transpose(0, 2, 1).reshape(nb * nh, 1, t)
    o_fox_p = _fox_attn(qn, kn, proj, c, c_t, nb, t, nh, blk(2 * fw), tq)

    lf_new = logf[mp:, :nh].reshape(nbs, 1, nh)
    bias = _page_bias(page_table, cache_logf, lf_new).reshape(nbs, page_table.shape[1], page * nh // LANES, LANES)
    v_all = proj[:, 2 * fw:3 * fw]
    o_fox_s = _paged_attn(page_table, qn[mp:].reshape(nbs, nh, dh), kn[mp:].reshape(nbs, nh, dh),
                          v_all[mp:].reshape(nbs, nh, dh), cache_k, cache_v, bias)

    o_hg_p, st_p = _hgrn_prompt(proj, lb, hg_out_norm, nb, t, nhg, blk(c_hq), blk(c_hf), blk(c_hi), blk(c_hg), chunk)
    o_hg_s, st_s = _hgrn_sample(proj, state_hgrn, lb, hg_out_norm, mp // group, nbs, nhg,
                                blk(c_hq), blk(c_hf), blk(c_hi), blk(c_hg), group)

    o_fox = jnp.concatenate([o_fox_p, o_fox_s.reshape(nbs, fw).astype(BF16)], axis=0)
    o_hg = jnp.concatenate([o_hg_p, o_hg_s.astype(BF16)], axis=0)
    merged = _merge(o_fox, o_hg, w_proj_fox[0].astype(BF16), w_proj_hg[0].astype(BF16), proj,
                    c_ga // tn, c_gb // tn, tm, tn)

    ne_pad = -(-n_experts // LANES) * LANES
    w_r = jnp.pad(w_router[0], ((0, 0), (0, ne_pad - n_experts))).astype(BF16)
    b_r = jnp.pad(b_router, ((0, 0), (0, ne_pad - n_experts)))
    h, hn, logits = _outproj(merged, w_out[0].astype(BF16), x, ffn_norm, w_r, b_r, tm_out, tn)

    pos, row_token, row_gate, block_expert, new_expert, valid = _route(logits[:, :n_experts], n_experts, moe_rows)
    xs = _gather_rows(row_token, hn, moe_rows)
    act = _moe_up(block_expert, new_expert, valid, xs, w_gate_up.reshape(w_gate_up.shape[1:]),
                  b_gate_up.reshape(n_experts, 1, -1), moe_rows, tf)
    y_rows = _moe_down(block_expert, new_expert, valid, act, w_down.reshape(w_down.shape[1:]),
                       b_down.reshape(n_experts, 1, -1), row_gate[:, None], moe_rows, tdn)
    tc = _divisor(m, 128, 8)
    h2, hn2 = _combine(pos, h, ple_norm, y_rows, tc)

    y = _ple(hn2, w_ple_gate[0].astype(BF16), p_all, w_ple_proj[0].astype(BF16), h2, tm, tn)

    y_prompt = y[:mp].reshape(nb, t, d)
    y_sample = y[mp:].reshape(nbs, 1, d)
    k_prompt = kn[:mp].reshape(1, nb, t, nh, dh)
    v_prompt = v_all[:mp].reshape(1, nb, t, nh, dh)
    logf_prompt = logf[:mp, :nh].reshape(1, nb, t, nh)
    k_sample = kn[mp:].reshape(1, nbs, 1, nh, dh)
    v_sample = v_all[mp:].reshape(1, nbs, 1, nh, dh)
    logf_sample = logf[mp:, :nh].reshape(1, nbs, 1, nh)
    return (y_prompt, y_sample, k_prompt, v_prompt, logf_prompt, st_p[None],
            k_sample, v_sample, logf_sample, st_s)
```

```python
import functools

import jax
import jax.numpy as jnp
from jax import lax
from jax.experimental import pallas as pl
from jax.experimental.pallas import tpu as pltpu

F32 = jnp.float32
BF16 = jnp.bfloat16

TOP_K = 4
SWIGLU_LIMIT = 7.0
SWIGLU_ALPHA = 1.702
RMS_EPS = 1e-6
LANES = 128
MASK_VALUE = -1e30
VMEM_LIMIT = 56 * 1024 * 1024


def _cparams(sem):
    return pltpu.CompilerParams(dimension_semantics=sem, vmem_limit_bytes=VMEM_LIMIT)


def _divisor(n, cap, mult):
    best = None
    for d in range(mult, min(n, cap) + 1, mult):
        if n % d == 0:
            best = d
    assert best is not None, (n, cap, mult)
    return best


def _dot(a, b):
    return jnp.dot(a, b, preferred_element_type=F32)


def _dot_nt(a, b):
    return lax.dot_general(a, b, (((1,), (1,)), ((), ())), preferred_element_type=F32)


def _dot_tn(a, b):
    return lax.dot_general(a, b, (((0,), (0,)), ((), ())), preferred_element_type=F32)


def _split3(x):
    x1 = x.astype(BF16)
    r1 = x - x1.astype(F32)
    x2 = r1.astype(BF16)
    x3 = (r1 - x2.astype(F32)).astype(BF16)
    return x1, x2, x3


def _sel_dot(sel, x):
    x1, x2, x3 = _split3(x)
    return _dot(sel, x1) + _dot(sel, x2) + _dot(sel, x3)


def _sigmoid(x):
    return 1.0 / (1.0 + jnp.exp(-x))


def _iota(shape, dim):
    return lax.broadcasted_iota(jnp.int32, shape, dim)


def _inproj_kernel(x_ref, g_ref, w_ref, o_ref, xn_ref):
    @pl.when(pl.program_id(1) == 0)
    def _():
        x = x_ref[...]
        ms = jnp.mean(x * x, axis=-1, keepdims=True)
        xn_ref[...] = (x * lax.rsqrt(ms + RMS_EPS) * g_ref[...]).astype(BF16)

    o_ref[...] = _dot(xn_ref[...], w_ref[...])


def _inproj(x, gain, w, tm, tn):
    m, d = x.shape
    n = w.shape[1]
    return pl.pallas_call(
        _inproj_kernel,
        grid=(m // tm, n // tn),
        in_specs=[pl.BlockSpec((tm, d), lambda i, j: (i, 0)),
                  pl.BlockSpec((1, d), lambda i, j: (0, 0)),
                  pl.BlockSpec((d, tn), lambda i, j: (0, j))],
        out_specs=pl.BlockSpec((tm, tn), lambda i, j: (i, j)),
        out_shape=jax.ShapeDtypeStruct((m, n), F32),
        scratch_shapes=[pltpu.VMEM((tm, d), BF16)],
        compiler_params=_cparams(("parallel", "arbitrary")),
        name="inproj",
    )(x, gain, w)


def _prep_kernel(fq_ref, fk_ref, fv_ref, ff_ref, qg_ref, kg_ref, fb_ref, q_ref, k_ref, v_ref, lf_ref, *, nh):
    qg = qg_ref[...]
    kg = kg_ref[...]
    for h in range(nh):
        sl = slice(h * LANES, (h + 1) * LANES)
        for src, gain, dst in ((fq_ref, qg, q_ref), (fk_ref, kg, k_ref)):
            xh = src[:, sl]
            ms = jnp.mean(xh * xh, axis=-1, keepdims=True)
            dst[:, sl] = xh * lax.rsqrt(ms + RMS_EPS) * gain
    v_ref[...] = fv_ref[...]
    z = ff_ref[...] + fb_ref[...]
    lf_ref[...] = jnp.minimum(z, 0.0) - jnp.log1p(jnp.exp(-jnp.abs(z)))


def _prep(proj, q_gain, k_gain, fbias, row0, n, tm, fw, ff_blk, ffw):
    nh = fw // LANES
    r0 = row0 // tm
    assert row0 % tm == 0 and n % tm == 0
    return pl.pallas_call(
        functools.partial(_prep_kernel, nh=nh),
        grid=(n // tm,),
        in_specs=[pl.BlockSpec((tm, fw), lambda i: (r0 + i, 0)),
                  pl.BlockSpec((tm, fw), lambda i: (r0 + i, 1)),
                  pl.BlockSpec((tm, fw), lambda i: (r0 + i, 2)),
                  pl.BlockSpec((tm, ffw), lambda i: (r0 + i, ff_blk)),
                  pl.BlockSpec((1, LANES), lambda i: (0, 0)),
                  pl.BlockSpec((1, LANES), lambda i: (0, 0)),
                  pl.BlockSpec((1, ffw), lambda i: (0, 0))],
        out_specs=[pl.BlockSpec((tm, fw), lambda i: (i, 0)),
                   pl.BlockSpec((tm, fw), lambda i: (i, 0)),
                   pl.BlockSpec((tm, fw), lambda i: (i, 0)),
                   pl.BlockSpec((tm, ffw), lambda i: (i, 0))],
        out_shape=[jax.ShapeDtypeStruct((n, fw), F32),
                   jax.ShapeDtypeStruct((n, fw), F32),
                   jax.ShapeDtypeStruct((n, fw), F32),
                   jax.ShapeDtypeStruct((n, ffw), F32)],
        compiler_params=_cparams(("parallel",)),
        name="fox_prep",
    )(proj, proj, proj, proj, q_gain, k_gain, fbias)


def _cumsum_kernel(x_ref, o_ref, *, t, c):
    tri = (_iota((c, c), 0) >= _iota((c, c), 1)).astype(BF16)
    carry = jnp.zeros((1, x_ref.shape[1]), F32)
    for n in range(t // c):
        blk = _sel_dot(tri, x_ref[n * c:(n + 1) * c, :]) + carry
        o_ref[n * c:(n + 1) * c, :] = blk
        carry = blk[c - 1:c, :]


def _seq_cumsum(x, nb, t):
    w = x.shape[1]
    return pl.pallas_call(
        functools.partial(_cumsum_kernel, t=t, c=LANES),
        grid=(nb,),
        in_specs=[pl.BlockSpec((t, w), lambda b: (b, 0))],
        out_specs=pl.BlockSpec((t, w), lambda b: (b, 0)),
        out_shape=jax.ShapeDtypeStruct((nb * t, w), F32),
        compiler_params=_cparams(("parallel",)),
        name="logf_cumsum",
    )(x)


def _fox_attn_kernel(q_ref, k_ref, v_ref, cc_ref, cr_ref, o_ref, *, tq, scale):
    h = pl.program_id(1)
    i = pl.program_id(2)
    q = q_ref[...].astype(BF16)
    lane = _iota(cc_ref.shape, 1)
    cq = jnp.sum(jnp.where(lane == h, cc_ref[...], 0.0), axis=-1, keepdims=True)
    causal = _iota((tq, tq), 0) >= _iota((tq, tq), 1)

    def block(j, carry, diagonal):
        m, l, acc = carry
        r0 = pl.multiple_of(j * tq, tq)
        ks = k_ref[pl.ds(r0, tq), :].astype(BF16)
        vs = v_ref[pl.ds(r0, tq), :].astype(BF16)
        ck = cr_ref[:, pl.ds(r0, tq)]
        s = _dot_nt(q, ks) * scale + (cq - ck)
        if diagonal:
            s = jnp.where(causal, s, MASK_VALUE)
        m_new = jnp.maximum(m, jnp.max(s, axis=-1, keepdims=True))
        alpha = jnp.exp(m - m_new)
        p = jnp.exp(s - m_new)
        l = alpha * l + jnp.sum(p, axis=-1, keepdims=True)
        acc = alpha * acc + _dot(p.astype(BF16), vs)
        return m_new, l, acc

    init = (jnp.full((tq, 1), MASK_VALUE, F32), jnp.zeros((tq, 1), F32), jnp.zeros((tq, LANES), F32))
    carry = lax.fori_loop(0, i, lambda j, c: block(j, c, False), init)
    _, l, acc = block(i, carry, True)
    o_ref[...] = (acc / l).astype(o_ref.dtype)


def _fox_attn(qn, kn, vn, c, c_t, nb, t, nh, tq):
    nq = t // tq
    cw = c.shape[1]
    return pl.pallas_call(
        functools.partial(_fox_attn_kernel, tq=tq, scale=LANES ** -0.5),
        grid=(nb, nh, nq),
        in_specs=[pl.BlockSpec((tq, LANES), lambda b, h, i: (b * nq + i, h)),
                  pl.BlockSpec((t, LANES), lambda b, h, i: (b, h)),
                  pl.BlockSpec((t, LANES), lambda b, h, i: (b, h)),
                  pl.BlockSpec((tq, cw), lambda b, h, i: (b * nq + i, 0)),
                  pl.BlockSpec((None, 1, t), lambda b, h, i: (b * nh + h, 0, 0))],
        out_specs=pl.BlockSpec((tq, LANES), lambda b, h, i: (b * nq + i, h)),
        out_shape=jax.ShapeDtypeStruct((nb * t, nh * LANES), BF16),
        compiler_params=_cparams(("parallel", "parallel", "arbitrary")),
        name="fox_attn",
    )(qn, kn, vn, c, c_t)


def _page_bias_kernel(pt_ref, *refs, n_pages, page):
    lf_refs = refs[:n_pages]
    new_ref = refs[n_pages]
    o_ref = refs[n_pages + 1]
    strict_upper = (_iota((page, page), 0) < _iota((page, page), 1)).astype(BF16)
    carry = new_ref[...]
    for p in range(n_pages - 1, -1, -1):
        x = lf_refs[p][...]
        suf = _sel_dot(strict_upper, x) + carry
        o_ref[p * page:(p + 1) * page, :] = suf
        carry = suf[0:1, :] + x[0:1, :]


def _page_bias(page_table, cache_logf, lf_new):
    nbs, n_pages = page_table.shape
    _, _, page, nh = cache_logf.shape
    in_specs = [pl.BlockSpec((None, None, page, nh),
                             functools.partial(lambda b, pt, p: (0, pt[b * n_pages + p], 0, 0), p=p))
                for p in range(n_pages)]
    in_specs.append(pl.BlockSpec((None, 1, nh), lambda b, pt: (b, 0, 0)))
    return pl.pallas_call(
        functools.partial(_page_bias_kernel, n_pages=n_pages, page=page),
        grid_spec=pltpu.PrefetchScalarGridSpec(
            num_scalar_prefetch=1,
            grid=(nbs,),
            in_specs=in_specs,
            out_specs=pl.BlockSpec((None, n_pages * page, nh), lambda b, pt: (b, 0, 0))),
        out_shape=jax.ShapeDtypeStruct((nbs, n_pages * page, nh), F32),
        compiler_params=_cparams(("arbitrary",)),
        name="page_bias",
    )(page_table.reshape(-1), *([cache_logf] * n_pages), lf_new)


def _paged_attn_kernel(pt_ref, q_ref, kn_ref, vn_ref, *refs, scale, page, nh, pps):
    k_refs, v_refs = refs[:pps], refs[pps:2 * pps]
    b_ref, o_ref, m_ref, l_ref, acc_ref = refs[2 * pps:]
    p = pl.program_id(1)
    q = q_ref[...]

    @pl.when(p == 0)
    def _():
        s_new = jnp.sum(q * kn_ref[...], axis=-1, keepdims=True) * scale
        m_ref[...] = jnp.broadcast_to(s_new, m_ref.shape)
        l_ref[...] = jnp.ones(l_ref.shape, F32)
        acc_ref[...] = vn_ref[...]

    rows = page * nh
    groups = rows // LANES
    eye = _iota((LANES, LANES), 0) == _iota((LANES, LANES), 1)
    ones = jnp.ones((2 * LANES, LANES), BF16)
    qs = q * scale
    scores = []
    for k_ref, g in zip(k_refs, range(pps)):
        kq = (k_ref[...] * qs[None]).reshape(groups, LANES, LANES)
        z = (kq + jnp.where(eye[None], b_ref[g][:, None, :], 0.0)).reshape(rows, LANES)
        hi = z.astype(BF16)
        lo = (z - hi.astype(F32)).astype(BF16)
        scores.append(_dot(jnp.concatenate([hi, lo], axis=1), ones).reshape(page, nh, LANES))
    m_old = m_ref[...]
    m_new = m_old
    for s in scores:
        m_new = jnp.maximum(m_new, jnp.max(s, axis=0))
    alpha = jnp.exp(m_old - m_new)
    l = alpha * l_ref[...]
    acc = alpha * acc_ref[...]
    for s, v_ref in zip(scores, v_refs):
        pr = jnp.exp(s - m_new[None])
        l = l + jnp.sum(pr, axis=0)
        acc = acc + jnp.sum(pr * v_ref[...], axis=0)
    l_ref[...] = l
    acc_ref[...] = acc
    m_ref[...] = m_new

    @pl.when(p == pl.num_programs(1) - 1)
    def _():
        o_ref[...] = acc_ref[...] / l_ref[...]


def _paged_attn(page_table, q, k_new, v_new, cache_k, cache_v, bias, pps):
    nbs, n_pages = page_table.shape
    _, _, page, nh, dh = cache_k.shape
    groups = page * nh // LANES
    tok = pl.BlockSpec((None, nh, dh), lambda b, p, pt: (b, 0, 0))
    pgs = [pl.BlockSpec((None, None, page, nh, dh),
                        functools.partial(lambda b, p, pt, g: (0, pt[b * n_pages + p * pps + g], 0, 0, 0), g=g))
           for g in range(pps)]
    return pl.pallas_call(
        functools.partial(_paged_attn_kernel, scale=dh ** -0.5, page=page, nh=nh, pps=pps),
        grid_spec=pltpu.PrefetchScalarGridSpec(
            num_scalar_prefetch=1,
            grid=(nbs, n_pages // pps),
            in_specs=[tok, tok, tok] + pgs + pgs +
                     [pl.BlockSpec((None, pps, groups, LANES), lambda b, p, pt: (b, p, 0, 0))],
            out_specs=tok,
            scratch_shapes=[pltpu.VMEM((nh, dh), F32)] * 3),
        out_shape=jax.ShapeDtypeStruct((nbs, nh, dh), F32),
        compiler_params=_cparams(("parallel", "arbitrary")),
        name="paged_attn",
    )(page_table.reshape(-1), q, k_new, v_new, *([cache_k] * pps), *([cache_v] * pps), bias)


def _hgrn_levels(c):
    levels = []
    m = 1
    while m < c:
        levels.append(m)
        m *= 2
    return levels


def _hgrn_prompt_kernel(hq_ref, hf_ref, hi_ref, hg_ref, lb_ref, gain_ref, o_ref, st_ref, *, t, c, hp):
    levels = _hgrn_levels(c)
    row = _iota((c, c), 0)
    col = _iota((c, c), 1)
    eye = row == col
    cums, masks = [(row >= col).astype(BF16)], []
    for m in levels:
        boundary = (row // (2 * m)) * (2 * m) + (m - 1)
        cums.append((col <= boundary).astype(BF16))
        same = (row // (2 * m)) == (col // (2 * m))
        masks.append(same & ((row % (2 * m)) >= m) & ((col % (2 * m)) < m))
    cum_all = jnp.concatenate(cums, axis=0)
    gain = gain_ref[...]

    def head_chunk(r0, hd, s_t):
        sl = slice(hd * LANES, (hd + 1) * LANES)
        lb = lb_ref[:, sl]
        hq = hq_ref[pl.ds(r0, c), sl]
        fg = lb + (1.0 - lb) * _sigmoid(hf_ref[pl.ds(r0, c), sl])
        q = hq * _sigmoid(hq)
        k = 1.0 - fg
        v = hi_ref[pl.ds(r0, c), sl].astype(BF16)
        sums = _sel_dot(cum_all, jnp.log(fg))
        b = sums[:c, :]
        a = jnp.where(eye, _dot_nt(q.astype(BF16), k.astype(BF16)), 0.0)
        for li in range(len(levels)):
            rl = sums[(li + 1) * c:(li + 2) * c, :]
            ql = q * jnp.exp(jnp.minimum(b - rl, 0.0))
            kl = k * jnp.exp(jnp.minimum(rl - b, 0.0))
            a = a + jnp.where(masks[li], _dot_nt(ql.astype(BF16), kl.astype(BF16)), 0.0)
        o = _dot(a.astype(BF16), v) + _dot_nt((q * jnp.exp(b)).astype(BF16), s_t.astype(BF16))
        b_last = b[c - 1:c, :]
        kd = k * jnp.exp(b_last - b)
        s_t = s_t * jnp.exp(b_last) + _dot_tn(v, kd.astype(BF16))
        ms = jnp.mean(o * o, axis=-1, keepdims=True)
        hg = hg_ref[pl.ds(r0, c), sl]
        o_ref[pl.ds(r0, c), sl] = (o * lax.rsqrt(ms + RMS_EPS) * gain * (hg * _sigmoid(hg))).astype(o_ref.dtype)
        return s_t

    def chunk(n, states):
        r0 = pl.multiple_of(n * c, c)
        return tuple(head_chunk(r0, hd, states[hd]) for hd in range(hp))

    states = lax.fori_loop(0, t // c, chunk, tuple(jnp.zeros((LANES, LANES), F32) for _ in range(hp)))
    ident = (_iota((LANES, LANES), 0) == _iota((LANES, LANES), 1)).astype(BF16)
    for hd in range(hp):
        s1, s2, s3 = _split3(states[hd])
        st_ref[hd] = _dot_nt(ident, s1) + _dot_nt(ident, s2) + _dot_nt(ident, s3)


def _hgrn_prompt(proj, lb, gain, nb, t, nh, blk_q, blk_f, blk_i, blk_g, c, hp):
    w = hp * LANES

    def col(blk):
        assert blk % hp == 0
        return pl.BlockSpec((t, w), lambda b, h: (b, blk // hp + h))
    return pl.pallas_call(
        functools.partial(_hgrn_prompt_kernel, t=t, c=c, hp=hp),
        grid=(nb, nh // hp),
        in_specs=[col(blk_q), col(blk_f), col(blk_i), col(blk_g),
                  pl.BlockSpec((1, w), lambda b, h: (0, h)),
                  pl.BlockSpec((1, LANES), lambda b, h: (0, 0))],
        out_specs=[pl.BlockSpec((t, w), lambda b, h: (b, h)),
                   pl.BlockSpec((None, hp, LANES, LANES), lambda b, h: (b, h, 0, 0))],
        out_shape=[jax.ShapeDtypeStruct((nb * t, nh * LANES), BF16),
                   jax.ShapeDtypeStruct((nb, nh, LANES, LANES), F32)],
        compiler_params=_cparams(("parallel", "parallel")),
        name="hgrn_prompt",
    )(proj, proj, proj, proj, lb, gain)


def _hgrn_sample_kernel(hq_ref, hf_ref, hi_ref, hg_ref, lb_ref, gain_ref, s_ref, o_ref, so_ref, *, group):
    eye = _iota((LANES, LANES), 0) == _iota((LANES, LANES), 1)

    def column(rowvec):
        return jnp.sum(jnp.where(eye, jnp.broadcast_to(rowvec, (LANES, LANES)), 0.0), axis=-1, keepdims=True)

    lb = lb_ref[...]
    gain = gain_ref[...]
    for s in range(group):
        hq = hq_ref[s:s + 1, :]
        fg = lb + (1.0 - lb) * _sigmoid(hf_ref[s:s + 1, :])
        q = hq * _sigmoid(hq)
        s_new = column(fg) * s_ref[s] + column(1.0 - fg) * hi_ref[s:s + 1, :]
        so_ref[s] = s_new
        o = jnp.sum(column(q) * s_new, axis=0, keepdims=True)
        ms = jnp.mean(o * o, axis=-1, keepdims=True)
        hg = hg_ref[s:s + 1, :]
        o_ref[s:s + 1, :] = o * lax.rsqrt(ms + RMS_EPS) * gain * (hg * _sigmoid(hg))


def _hgrn_sample(proj, state, lb, gain, row_blk0, nbs, nh, blk_q, blk_f, blk_i, blk_g, group):
    def col(blk):
        return pl.BlockSpec((group, LANES), lambda h, i: (row_blk0 + i, blk + h))
    st = pl.BlockSpec((None, group, None, LANES, LANES), lambda h, i: (0, i, h, 0, 0))
    return pl.pallas_call(
        functools.partial(_hgrn_sample_kernel, group=group),
        grid=(nh, nbs // group),
        in_specs=[col(blk_q), col(blk_f), col(blk_i), col(blk_g),
                  pl.BlockSpec((1, LANES), lambda h, i: (0, h)),
                  pl.BlockSpec((1, LANES), lambda h, i: (0, 0)),
                  st],
        out_specs=[pl.BlockSpec((group, LANES), lambda h, i: (i, h)), st],
        out_shape=[jax.ShapeDtypeStruct((nbs, nh * LANES), F32),
                   jax.ShapeDtypeStruct((1,) + state.shape[1:], F32)],
        compiler_params=_cparams(("parallel", "parallel")),
        name="hgrn_sample",
    )(proj, proj, proj, proj, lb, gain, state)


def _merge_kernel(of_ref, oh_ref, wf_ref, wh_ref, ga_ref, gb_ref, o_ref):
    a = _dot(of_ref[...], wf_ref[...])
    b = _dot(oh_ref[...], wh_ref[...])
    o_ref[...] = (_sigmoid(ga_ref[...]) * a + _sigmoid(gb_ref[...]) * b).astype(o_ref.dtype)


def _merge(o_fox, o_hg, wf, wh, proj, blk_ga, blk_gb, tm, tn):
    m, kf = o_fox.shape
    n = wf.shape[1]
    return pl.pallas_call(
        _merge_kernel,
        grid=(m // tm, n // tn),
        in_specs=[pl.BlockSpec((tm, kf), lambda i, j: (i, 0)),
                  pl.BlockSpec((tm, kf), lambda i, j: (i, 0)),
                  pl.BlockSpec((kf, tn), lambda i, j: (0, j)),
                  pl.BlockSpec((kf, tn), lambda i, j: (0, j)),
                  pl.BlockSpec((tm, tn), lambda i, j: (i, blk_ga + j)),
                  pl.BlockSpec((tm, tn), lambda i, j: (i, blk_gb + j))],
        out_specs=pl.BlockSpec((tm, tn), lambda i, j: (i, j)),
        out_shape=jax.ShapeDtypeStruct((m, n), BF16),
        compiler_params=_cparams(("parallel", "arbitrary")),
        name="branch_merge",
    )(o_fox, o_hg, wf, wh, proj, proj)


def _outproj_kernel(mg_ref, w_ref, x_ref, g_ref, wr_ref, br_ref, h_ref, hn_ref, lg_ref, hrow_ref, *, tn):
    j = pl.program_id(1)
    hblk = x_ref[...] + _dot(mg_ref[...], w_ref[...])
    h_ref[...] = hblk
    c0 = pl.multiple_of(j * tn, tn)
    hrow_ref[:, pl.ds(c0, tn)] = hblk

    @pl.when(j == pl.num_programs(1) - 1)
    def _():
        h = hrow_ref[...]
        ms = jnp.mean(h * h, axis=-1, keepdims=True)
        hn = h * lax.rsqrt(ms + RMS_EPS) * g_ref[...]
        hn_ref[...] = hn
        lg_ref[...] = _dot(hn.astype(BF16), wr_ref[...]) + br_ref[...]


def _outproj(merged, w_out, x, gain, w_router, b_router, tm, tn):
    m, d = x.shape
    ne = w_router.shape[1]
    return pl.pallas_call(
        functools.partial(_outproj_kernel, tn=tn),
        grid=(m // tm, d // tn),
        in_specs=[pl.BlockSpec((tm, d), lambda i, j: (i, 0)),
                  pl.BlockSpec((d, tn), lambda i, j: (0, j)),
                  pl.BlockSpec((tm, tn), lambda i, j: (i, j)),
                  pl.BlockSpec((1, d), lambda i, j: (0, 0)),
                  pl.BlockSpec((d, ne), lambda i, j: (0, 0)),
                  pl.BlockSpec((1, ne), lambda i, j: (0, 0))],
        out_specs=[pl.BlockSpec((tm, tn), lambda i, j: (i, j)),
                   pl.BlockSpec((tm, d), lambda i, j: (i, 0)),
                   pl.BlockSpec((tm, ne), lambda i, j: (i, 0))],
        out_shape=[jax.ShapeDtypeStruct((m, d), F32),
                   jax.ShapeDtypeStruct((m, d), F32),
                   jax.ShapeDtypeStruct((m, ne), F32)],
        scratch_shapes=[pltpu.VMEM((tm, d), F32)],
        compiler_params=_cparams(("parallel", "arbitrary")),
        name="out_proj",
    )(merged, w_out, x, gain, w_router, b_router)


def _gather_kernel(tok_ref, va_ref, x_hbm, o_ref, buf, sem, *, rows):
    i = pl.program_id(0)
    last = pl.num_programs(0) - 1

    def issue(blk, slot):
        def start(r, carry):
            t = tok_ref[blk * rows + r]
            pltpu.make_async_copy(x_hbm.at[pl.ds(t, 1), :], buf.at[slot, pl.ds(r, 1), :], sem.at[slot]).start()
            return carry
        lax.fori_loop(0, rows, start, 0, unroll=8)

    @pl.when(jnp.logical_and(i == 0, va_ref[0] == 1))
    def _():
        issue(0, 0)

    nxt = jnp.minimum(i + 1, last)

    @pl.when(jnp.logical_and(i < last, va_ref[nxt] == 1))
    def _():
        issue(nxt, nxt % 2)

    slot = i % 2

    @pl.when(va_ref[i] == 1)
    def _():
        pltpu.make_async_copy(x_hbm.at[pl.ds(0, rows), :], buf.at[slot], sem.at[slot]).wait()
        o_ref[...] = buf[slot].astype(o_ref.dtype)

    @pl.when(va_ref[i] == 0)
    def _():
        o_ref[...] = jnp.zeros(o_ref.shape, o_ref.dtype)


def _gather_rows(row_token, valid, x, rows):
    n_rows = row_token.shape[0]
    d = x.shape[1]
    assert x.shape[0] >= rows
    return pl.pallas_call(
        functools.partial(_gather_kernel, rows=rows),
        grid_spec=pltpu.PrefetchScalarGridSpec(
            num_scalar_prefetch=2,
            grid=(n_rows // rows,),
            in_specs=[pl.BlockSpec(memory_space=pl.ANY)],
            out_specs=pl.BlockSpec((rows, d), lambda i, tok, va: (i, 0)),
            scratch_shapes=[pltpu.VMEM((2, rows, d), x.dtype), pltpu.SemaphoreType.DMA((2,))]),
        out_shape=jax.ShapeDtypeStruct((n_rows, d), BF16),
        compiler_params=_cparams(("arbitrary",)),
        name="moe_gather",
    )(row_token, valid, x)


def _weight_copy(w_hbm, e, col, width, stage, sem):
    return pltpu.make_async_copy(w_hbm.at[e, :, pl.ds(pl.multiple_of(col, LANES), width)], stage, sem)


def _stream_weights(ne_ref, nx_ref, lr_ref, be_ref, copies, convert):
    j = pl.program_id(0)
    i = pl.program_id(1)

    @pl.when(ne_ref[i] == 1)
    def _():
        cur = copies(be_ref[i], j)

        @pl.when(jnp.logical_and(j == 0, i == 0))
        def _():
            for c in cur:
                c.start()

        for c in cur:
            c.wait()
        convert()
        jn = jnp.where(lr_ref[i] == 1, j + 1, j)

        @pl.when(jn < pl.num_programs(0))
        def _():
            for c in copies(nx_ref[i], jn):
                c.start()


def _moe_up_kernel(be_ref, ne_ref, va_ref, nx_ref, lr_ref, x_ref, w_hbm, bg_ref, bu_ref, o_ref,
                   sg, su, wgb, wub, sem, *, tf, f):
    i = pl.program_id(1)

    def copies(e, jj):
        return (_weight_copy(w_hbm, e, jj * tf, tf, sg, sem.at[0]),
                _weight_copy(w_hbm, e, f + jj * tf, tf, su, sem.at[1]))

    def convert():
        wgb[...] = sg[...].astype(BF16)
        wub[...] = su[...].astype(BF16)

    _stream_weights(ne_ref, nx_ref, lr_ref, be_ref, copies, convert)

    @pl.when(va_ref[i] == 1)
    def _():
        x = x_ref[...]
        g = jnp.minimum(_dot(x, wgb[...]) + bg_ref[...], SWIGLU_LIMIT)
        u = jnp.clip(_dot(x, wub[...]) + bu_ref[...], -SWIGLU_LIMIT, SWIGLU_LIMIT)
        o_ref[...] = ((u + 1.0) * g * _sigmoid(SWIGLU_ALPHA * g)).astype(o_ref.dtype)

    @pl.when(va_ref[i] == 0)
    def _():
        o_ref[...] = jnp.zeros(o_ref.shape, o_ref.dtype)


def _moe_up(sched, xs, w_gate_up, b_gate_up, rows, tf):
    n_rows, d = xs.shape
    f = w_gate_up.shape[2] // 2
    nf = f // tf
    return pl.pallas_call(
        functools.partial(_moe_up_kernel, tf=tf, f=f),
        grid_spec=pltpu.PrefetchScalarGridSpec(
            num_scalar_prefetch=5,
            grid=(nf, n_rows // rows),
            in_specs=[pl.BlockSpec((rows, d), lambda j, i, be, *_: (i, 0)),
                      pl.BlockSpec(memory_space=pl.ANY),
                      pl.BlockSpec((None, 1, tf), lambda j, i, be, *_: (be[i], 0, j)),
                      pl.BlockSpec((None, 1, tf), lambda j, i, be, *_: (be[i], 0, nf + j))],
            out_specs=pl.BlockSpec((rows, tf), lambda j, i, be, *_: (i, j)),
            scratch_shapes=[pltpu.VMEM((d, tf), F32), pltpu.VMEM((d, tf), F32),
                            pltpu.VMEM((d, tf), BF16), pltpu.VMEM((d, tf), BF16),
                            pltpu.SemaphoreType.DMA((2,))]),
        out_shape=jax.ShapeDtypeStruct((n_rows, f), BF16),
        compiler_params=_cparams(("arbitrary", "arbitrary")),
        name="moe_up",
    )(*sched, xs, w_gate_up, b_gate_up, b_gate_up)


def _moe_down_kernel(be_ref, ne_ref, va_ref, nx_ref, lr_ref, a_ref, w_hbm, b_ref, o_ref, sw, wb, sem, *, tn):
    i = pl.program_id(1)

    def copies(e, jj):
        return (_weight_copy(w_hbm, e, jj * tn, tn, sw, sem.at[0]),)

    def convert():
        wb[...] = sw[...].astype(BF16)

    _stream_weights(ne_ref, nx_ref, lr_ref, be_ref, copies, convert)

    @pl.when(va_ref[i] == 1)
    def _():
        o_ref[...] = _dot(a_ref[...], wb[...]) + b_ref[...]

    @pl.when(va_ref[i] == 0)
    def _():
        o_ref[...] = jnp.zeros(o_ref.shape, o_ref.dtype)


def _moe_down(sched, act, w_down, b_down, rows, tn):
    n_rows, f = act.shape
    d = w_down.shape[2]
    return pl.pallas_call(
        functools.partial(_moe_down_kernel, tn=tn),
        grid_spec=pltpu.PrefetchScalarGridSpec(
            num_scalar_prefetch=5,
            grid=(d // tn, n_rows // rows),
            in_specs=[pl.BlockSpec((rows, f), lambda j, i, be, *_: (i, 0)),
                      pl.BlockSpec(memory_space=pl.ANY),
                      pl.BlockSpec((None, 1, tn), lambda j, i, be, *_: (be[i], 0, j))],
            out_specs=pl.BlockSpec((rows, tn), lambda j, i, be, *_: (i, j)),
            scratch_shapes=[pltpu.VMEM((f, tn), F32), pltpu.VMEM((f, tn), BF16),
                            pltpu.SemaphoreType.DMA((1,))]),
        out_shape=jax.ShapeDtypeStruct((n_rows, d), F32),
        compiler_params=_cparams(("arbitrary", "arbitrary")),
        name="moe_down",
    )(*sched, act, w_down, b_down)


def _combine_kernel(pos_ref, h_ref, gt_ref, g_ref, y_hbm, h2_ref, hn_ref, buf, sem, *, tm, top_k):
    i = pl.program_id(0)
    last = pl.num_programs(0) - 1
    n = tm * top_k

    def issue(blk, slot):
        def start(a, carry):
            src = pos_ref[blk * n + a]
            pltpu.make_async_copy(y_hbm.at[pl.ds(src, 1), :],
                                  buf.at[slot, a % top_k, pl.ds(a // top_k, 1), :], sem.at[slot]).start()
            return carry
        lax.fori_loop(0, n, start, 0, unroll=8)

    @pl.when(i == 0)
    def _():
        issue(0, 0)

    nxt = jnp.minimum(i + 1, last)

    @pl.when(i < last)
    def _():
        issue(nxt, nxt % 2)

    slot = i % 2
    for k in range(top_k):
        pltpu.make_async_copy(y_hbm.at[pl.ds(0, tm), :], buf.at[slot, k], sem.at[slot]).wait()
    h = h_ref[...]
    gt = gt_ref[...]
    for k in range(top_k):
        h = h + gt[:, k:k + 1] * buf[slot, k]
    h2_ref[...] = h
    ms = jnp.mean(h * h, axis=-1, keepdims=True)
    hn_ref[...] = (h * lax.rsqrt(ms + RMS_EPS) * g_ref[...]).astype(hn_ref.dtype)


def _combine(pos, h, gates, gain, y_rows, tm):
    m, d = h.shape
    assert y_rows.shape[0] >= tm
    return pl.pallas_call(
        functools.partial(_combine_kernel, tm=tm, top_k=TOP_K),
        grid_spec=pltpu.PrefetchScalarGridSpec(
            num_scalar_prefetch=1,
            grid=(m // tm,),
            in_specs=[pl.BlockSpec((tm, d), lambda i, pos: (i, 0)),
                      pl.BlockSpec((tm, TOP_K), lambda i, pos: (i, 0)),
                      pl.BlockSpec((1, d), lambda i, pos: (0, 0)),
                      pl.BlockSpec(memory_space=pl.ANY)],
            out_specs=[pl.BlockSpec((tm, d), lambda i, pos: (i, 0)),
                       pl.BlockSpec((tm, d), lambda i, pos: (i, 0))],
            scratch_shapes=[pltpu.VMEM((2, TOP_K, tm, d), F32), pltpu.SemaphoreType.DMA((2,))]),
        out_shape=[jax.ShapeDtypeStruct((m, d), F32), jax.ShapeDtypeStruct((m, d), BF16)],
        compiler_params=_cparams(("arbitrary",)),
        name="moe_combine",
    )(pos, h, gates, gain, y_rows)


def _ple_kernel(hn_ref, wg_ref, p_ref, wp_ref, h_ref, o_ref):
    gate = _sigmoid(_dot(hn_ref[...], wg_ref[...]))
    o_ref[...] = h_ref[...] + gate * _dot(p_ref[...], wp_ref[...])


def _ple(hn, w_gate, p, w_proj, h, row0, n, tm, tn):
    d = h.shape[1]
    pd = p.shape[1]
    r0 = row0 // tm
    assert row0 % tm == 0 and n % tm == 0
    return pl.pallas_call(
        _ple_kernel,
        grid=(n // tm, d // tn),
        in_specs=[pl.BlockSpec((tm, d), lambda i, j: (r0 + i, 0)),
                  pl.BlockSpec((d, tn), lambda i, j: (0, j)),
                  pl.BlockSpec((tm, pd), lambda i, j: (i, 0)),
                  pl.BlockSpec((pd, tn), lambda i, j: (0, j)),
                  pl.BlockSpec((tm, tn), lambda i, j: (r0 + i, j))],
        out_specs=pl.BlockSpec((tm, tn), lambda i, j: (i, j)),
        out_shape=jax.ShapeDtypeStruct((n, d), F32),
        compiler_params=_cparams(("parallel", "arbitrary")),
        name="ple",
    )(hn, w_gate, p, w_proj, h)


def _route(logits, n_experts, rows):
    n = logits.shape[0]
    n_assign = n * TOP_K
    n_blocks = (n_assign + n_experts * (rows - 1) + rows - 1) // rows
    n_rows = n_blocks * rows
    top_logits, top_idx = lax.top_k(logits, TOP_K)
    gates = jax.nn.softmax(top_logits, axis=-1)
    expert = top_idx.reshape(-1).astype(jnp.int32)
    token = jnp.arange(n_assign, dtype=jnp.int32) // TOP_K
    onehot = expert[:, None] == jnp.arange(n_experts, dtype=jnp.int32)[None, :]
    cb = _divisor(n_assign, 256, 8)
    oh = onehot.reshape(n_assign // cb, cb, n_experts).astype(BF16)
    tri = (jnp.arange(cb)[:, None] >= jnp.arange(cb)[None, :]).astype(BF16)
    within = jnp.einsum('ij,bjk->bik', tri, oh, preferred_element_type=F32)
    totals = within[:, -1, :]
    offsets = jnp.cumsum(totals, axis=0) - totals
    csum = (within + offsets[:, None, :]).reshape(n_assign, n_experts)
    rank = jnp.sum(jnp.where(onehot, csum, 0.0), axis=1).astype(jnp.int32) - 1
    counts = (offsets[-1] + totals[-1]).astype(jnp.int32)
    padded = (counts + rows - 1) // rows * rows
    pend = jnp.cumsum(padded)
    pos = (pend - padded)[expert] + rank
    row_token = jnp.zeros((n_rows,), jnp.int32).at[pos].set(token)
    blk_start = jnp.arange(n_blocks, dtype=jnp.int32) * rows
    block_expert = jnp.minimum(jnp.searchsorted(pend, blk_start, side='right'), n_experts - 1).astype(jnp.int32)
    valid = (blk_start < pend[-1]).astype(jnp.int32)
    last_valid = block_expert[jnp.maximum(pend[-1] // rows - 1, 0)]
    block_expert = jnp.where(valid == 1, block_expert, last_valid)
    prev = jnp.concatenate([jnp.full((1,), -1, jnp.int32), block_expert[:-1]])
    new_expert = (block_expert != prev).astype(jnp.int32)
    after = jnp.searchsorted(block_expert, block_expert, side='right')
    last_run = (after >= n_blocks).astype(jnp.int32)
    next_expert = block_expert[jnp.where(after >= n_blocks, 0, after)]
    sched = (block_expert, new_expert, valid, next_expert, last_run)
    return pos.astype(jnp.int32), gates, row_token, sched


def kernel(x_prompt, x_sample, cache_k, cache_v, cache_logf, state_hgrn, page_table, p_prompt, p_sample,
           attn_norm, w_in, fox_fgate_bias, q_norm, k_norm, hg_lower_bound, hg_out_norm, w_proj_fox,
           w_proj_hg, w_out, ffn_norm, w_router, b_router, w_gate_up, b_gate_up, w_down, b_down,
           ple_norm, w_ple_gate, w_ple_proj):
    nb, t, d = x_prompt.shape
    nbs = x_sample.shape[0]
    assert x_sample.shape[1] == 1 and cache_k.shape[0] == 1
    _, _, page, nh, dh = cache_k.shape
    assert dh == LANES
    fw = nh * dh
    nhg = state_hgrn.shape[2]
    hw = nhg * LANES
    assert state_hgrn.shape[3] == LANES and state_hgrn.shape[4] == LANES and hw == fw
    n_experts = w_router.shape[2]
    mp = nb * t
    m = mp + nbs

    tm = _divisor(m, 704, 16)
    tn = _divisor(fw, 512, LANES)
    tq = _divisor(t, 512, LANES)
    chunk = LANES
    group = 8
    ts = _divisor(nbs, 128, 16)
    tp = _divisor(mp, 512, 16)
    hp = 4 if nhg % 4 == 0 and (3 * fw // LANES) % 4 == 0 else (2 if nhg % 2 == 0 else 1)
    pps = _divisor(page_table.shape[1], 4, 1)
    moe_rows = 256 if m * TOP_K // n_experts >= 512 else 16
    tm_out = _divisor(m, 320, 16)
    tf = _divisor(d, 512, LANES)
    tdn = _divisor(d, 1024, LANES)
    assert nbs % group == 0 and mp % group == 0 and t % chunk == 0

    x = jnp.concatenate([x_prompt.reshape(mp, d), x_sample.reshape(nbs, d)], axis=0)
    w = w_in[0]
    ff0 = 3 * fw
    w_al = jnp.concatenate([w[:, :ff0], w[:, ff0 + nh:],
                            jnp.pad(w[:, ff0:ff0 + nh], ((0, 0), (0, tn - nh)))], axis=1).astype(BF16)
    blk = lambda col: col // LANES
    c_hq, c_hf, c_hi, c_hg = ff0, ff0 + hw, ff0 + 2 * hw, ff0 + 3 * hw
    c_ga = ff0 + 4 * hw
    c_gb = c_ga + d
    c_ff = c_gb + d
    lb = jax.nn.softmax(hg_lower_bound.astype(F32), axis=0)[0:1]
    fbias = jnp.pad(fox_fgate_bias, ((0, 0), (0, LANES - nh)))

    proj = _inproj(x, attn_norm, w_al, tm, tn)
    qn_p, kn_p, vn_p, logf_p = _prep(proj, q_norm, k_norm, fbias, 0, mp, tp, fw, blk(c_ff), LANES)
    qn_s, kn_s, vn_s, logf_s = _prep(proj, q_norm, k_norm, fbias, mp, nbs, ts, fw, blk(c_ff), LANES)

    c = _seq_cumsum(logf_p, nb, t)
    c_t = c[:, :nh].reshape(nb, t, nh).transpose(0, 2, 1).reshape(nb * nh, 1, t)
    o_fox_p = _fox_attn(qn_p, kn_p, vn_p, c, c_t, nb, t, nh, tq)

    lf_new = logf_s[:, :nh].reshape(nbs, 1, nh)
    bias = _page_bias(page_table, cache_logf, lf_new).reshape(nbs, page_table.shape[1], page * nh // LANES, LANES)
    o_fox_s = _paged_attn(page_table, qn_s.reshape(nbs, nh, dh), kn_s.reshape(nbs, nh, dh),
                          vn_s.reshape(nbs, nh, dh), cache_k, cache_v, bias, pps)

    o_hg_p, st_p = _hgrn_prompt(proj, lb, hg_out_norm, nb, t, nhg, blk(c_hq), blk(c_hf), blk(c_hi), blk(c_hg),
                                chunk, hp)
    o_hg_s, st_s = _hgrn_sample(proj, state_hgrn, lb, hg_out_norm, mp // group, nbs, nhg,
                                blk(c_hq), blk(c_hf), blk(c_hi), blk(c_hg), group)

    o_fox = jnp.concatenate([o_fox_p, o_fox_s.reshape(nbs, fw).astype(BF16)], axis=0)
    o_hg = jnp.concatenate([o_hg_p, o_hg_s.astype(BF16)], axis=0)
    merged = _merge(o_fox, o_hg, w_proj_fox[0].astype(BF16), w_proj_hg[0].astype(BF16), proj,
                    c_ga // tn, c_gb // tn, tm, tn)

    ne_pad = -(-n_experts // LANES) * LANES
    w_r = jnp.pad(w_router[0], ((0, 0), (0, ne_pad - n_experts))).astype(BF16)
    b_r = jnp.pad(b_router, ((0, 0), (0, ne_pad - n_experts)))
    h, hn, logits = _outproj(merged, w_out[0].astype(BF16), x, ffn_norm, w_r, b_r, tm_out, tn)

    pos, gates, row_token, sched = _route(logits[:, :n_experts], n_experts, moe_rows)
    xs = _gather_rows(row_token, sched[2], hn, moe_rows)
    act = _moe_up(sched, xs, w_gate_up.reshape(w_gate_up.shape[1:]), b_gate_up.reshape(n_experts, 1, -1),
                  moe_rows, tf)
    y_rows = _moe_down(sched, act, w_down.reshape(w_down.shape[1:]), b_down.reshape(n_experts, 1, -1),
                       moe_rows, tdn)
    h2, hn2 = _combine(pos, h, gates, ple_norm, y_rows, _divisor(m, 128, 16))

    w_pg = w_ple_gate[0].astype(BF16)
    w_pp = w_ple_proj[0].astype(BF16)
    y_p = _ple(hn2, w_pg, p_prompt[0].reshape(mp, -1).astype(BF16), w_pp, h2, 0, mp, tp, tn)
    y_s = _ple(hn2, w_pg, p_sample[0].reshape(nbs, -1).astype(BF16), w_pp, h2, mp, nbs, ts, tn)

    return (y_p.reshape(nb, t, d), y_s.reshape(nbs, 1, d),
            kn_p.reshape(1, nb, t, nh, dh), vn_p.reshape(1, nb, t, nh, dh), logf_p[:, :nh].reshape(1, nb, t, nh),
            st_p[None],
            kn_s.reshape(1, nbs, 1, nh, dh), vn_s.reshape(1, nbs, 1, nh, dh), logf_s[:, :nh].reshape(1, nbs, 1, nh),
            st_s)
```

```python
import functools

import jax
import jax.numpy as jnp
from jax import lax
from jax.experimental import pallas as pl
from jax.experimental.pallas import tpu as pltpu

F32 = jnp.float32
BF16 = jnp.bfloat16

TOP_K = 4
SWIGLU_LIMIT = 7.0
SWIGLU_ALPHA = 1.702
RMS_EPS = 1e-6
LANES = 128
MASK_VALUE = -1e30
VMEM_LIMIT = 56 * 1024 * 1024


def _cparams(sem):
    return pltpu.CompilerParams(dimension_semantics=sem, vmem_limit_bytes=VMEM_LIMIT)


def _divisor(n, cap, mult):
    best = None
    for d in range(mult, min(n, cap) + 1, mult):
        if n % d == 0:
            best = d
    assert best is not None, (n, cap, mult)
    return best


def _dot(a, b):
    return jnp.dot(a, b, preferred_element_type=F32)


def _dot_nt(a, b):
    return lax.dot_general(a, b, (((1,), (1,)), ((), ())), preferred_element_type=F32)


def _dot_tn(a, b):
    return lax.dot_general(a, b, (((0,), (0,)), ((), ())), preferred_element_type=F32)


def _split3(x):
    x1 = x.astype(BF16)
    r1 = x - x1.astype(F32)
    x2 = r1.astype(BF16)
    x3 = (r1 - x2.astype(F32)).astype(BF16)
    return x1, x2, x3


def _sel_dot(sel, x):
    x1, x2, x3 = _split3(x)
    return _dot(sel, x1) + _dot(sel, x2) + _dot(sel, x3)


def _sigmoid(x):
    return 1.0 / (1.0 + jnp.exp(-x))


def _iota(shape, dim):
    return lax.broadcasted_iota(jnp.int32, shape, dim)


def _inproj_kernel(x_ref, g_ref, w_ref, o_ref, xn_ref):
    @pl.when(pl.program_id(1) == 0)
    def _():
        x = x_ref[...]
        ms = jnp.mean(x * x, axis=-1, keepdims=True)
        xn_ref[...] = (x * lax.rsqrt(ms + RMS_EPS) * g_ref[...]).astype(BF16)

    o_ref[...] = _dot(xn_ref[...], w_ref[...])


def _inproj(x, gain, w, tm, tn):
    m, d = x.shape
    n = w.shape[1]
    return pl.pallas_call(
        _inproj_kernel,
        grid=(m // tm, n // tn),
        in_specs=[pl.BlockSpec((tm, d), lambda i, j: (i, 0)),
                  pl.BlockSpec((1, d), lambda i, j: (0, 0)),
                  pl.BlockSpec((d, tn), lambda i, j: (0, j))],
        out_specs=pl.BlockSpec((tm, tn), lambda i, j: (i, j)),
        out_shape=jax.ShapeDtypeStruct((m, n), F32),
        scratch_shapes=[pltpu.VMEM((tm, d), BF16)],
        compiler_params=_cparams(("parallel", "arbitrary")),
        name="inproj",
    )(x, gain, w)


def _prep_kernel(fq_ref, fk_ref, fv_ref, ff_ref, qg_ref, kg_ref, fb_ref, q_ref, k_ref, v_ref, lf_ref, *, nh):
    qg = qg_ref[...]
    kg = kg_ref[...]
    for h in range(nh):
        sl = slice(h * LANES, (h + 1) * LANES)
        for src, gain, dst in ((fq_ref, qg, q_ref), (fk_ref, kg, k_ref)):
            xh = src[:, sl]
            ms = jnp.mean(xh * xh, axis=-1, keepdims=True)
            dst[:, sl] = xh * lax.rsqrt(ms + RMS_EPS) * gain
    v_ref[...] = fv_ref[...]
    z = ff_ref[...] + fb_ref[...]
    lf_ref[...] = jnp.minimum(z, 0.0) - jnp.log1p(jnp.exp(-jnp.abs(z)))


def _prep(proj, q_gain, k_gain, fbias, row0, n, tm, fw, ff_blk, ffw):
    nh = fw // LANES
    r0 = row0 // tm
    assert row0 % tm == 0 and n % tm == 0
    return pl.pallas_call(
        functools.partial(_prep_kernel, nh=nh),
        grid=(n // tm,),
        in_specs=[pl.BlockSpec((tm, fw), lambda i: (r0 + i, 0)),
                  pl.BlockSpec((tm, fw), lambda i: (r0 + i, 1)),
                  pl.BlockSpec((tm, fw), lambda i: (r0 + i, 2)),
                  pl.BlockSpec((tm, ffw), lambda i: (r0 + i, ff_blk)),
                  pl.BlockSpec((1, LANES), lambda i: (0, 0)),
                  pl.BlockSpec((1, LANES), lambda i: (0, 0)),
                  pl.BlockSpec((1, ffw), lambda i: (0, 0))],
        out_specs=[pl.BlockSpec((tm, fw), lambda i: (i, 0)),
                   pl.BlockSpec((tm, fw), lambda i: (i, 0)),
                   pl.BlockSpec((tm, fw), lambda i: (i, 0)),
                   pl.BlockSpec((tm, ffw), lambda i: (i, 0))],
        out_shape=[jax.ShapeDtypeStruct((n, fw), F32),
                   jax.ShapeDtypeStruct((n, fw), F32),
                   jax.ShapeDtypeStruct((n, fw), F32),
                   jax.ShapeDtypeStruct((n, ffw), F32)],
        compiler_params=_cparams(("parallel",)),
        name="fox_prep",
    )(proj, proj, proj, proj, q_gain, k_gain, fbias)


def _cumsum_kernel(x_ref, o_ref, *, t, c):
    tri = (_iota((c, c), 0) >= _iota((c, c), 1)).astype(BF16)
    carry = jnp.zeros((1, x_ref.shape[1]), F32)
    for n in range(t // c):
        blk = _sel_dot(tri, x_ref[n * c:(n + 1) * c, :]) + carry
        o_ref[n * c:(n + 1) * c, :] = blk
        carry = blk[c - 1:c, :]


def _seq_cumsum(x, nb, t):
    w = x.shape[1]
    return pl.pallas_call(
        functools.partial(_cumsum_kernel, t=t, c=LANES),
        grid=(nb,),
        in_specs=[pl.BlockSpec((t, w), lambda b: (b, 0))],
        out_specs=pl.BlockSpec((t, w), lambda b: (b, 0)),
        out_shape=jax.ShapeDtypeStruct((nb * t, w), F32),
        compiler_params=_cparams(("parallel",)),
        name="logf_cumsum",
    )(x)


def _fox_attn_kernel(q_ref, k_ref, v_ref, cc_ref, cr_ref, o_ref, *, tq, scale):
    h = pl.program_id(1)
    i = pl.program_id(2)
    q = q_ref[...].astype(BF16)
    lane = _iota(cc_ref.shape, 1)
    cq = jnp.sum(jnp.where(lane == h, cc_ref[...], 0.0), axis=-1, keepdims=True)
    causal = _iota((tq, tq), 0) >= _iota((tq, tq), 1)

    def block(j, carry, diagonal):
        m, l, acc = carry
        r0 = pl.multiple_of(j * tq, tq)
        ks = k_ref[pl.ds(r0, tq), :].astype(BF16)
        vs = v_ref[pl.ds(r0, tq), :].astype(BF16)
        ck = cr_ref[:, pl.ds(r0, tq)]
        s = _dot_nt(q, ks) * scale + (cq - ck)
        if diagonal:
            s = jnp.where(causal, s, MASK_VALUE)
        m_new = jnp.maximum(m, jnp.max(s, axis=-1, keepdims=True))
        alpha = jnp.exp(m - m_new)
        p = jnp.exp(s - m_new)
        l = alpha * l + jnp.sum(p, axis=-1, keepdims=True)
        acc = alpha * acc + _dot(p.astype(BF16), vs)
        return m_new, l, acc

    init = (jnp.full((tq, 1), MASK_VALUE, F32), jnp.zeros((tq, 1), F32), jnp.zeros((tq, LANES), F32))
    carry = lax.fori_loop(0, i, lambda j, c: block(j, c, False), init)
    _, l, acc = block(i, carry, True)
    o_ref[...] = (acc / l).astype(o_ref.dtype)


def _fox_attn(qn, kn, vn, c, c_t, nb, t, nh, tq):
    nq = t // tq
    cw = c.shape[1]
    return pl.pallas_call(
        functools.partial(_fox_attn_kernel, tq=tq, scale=LANES ** -0.5),
        grid=(nb, nh, nq),
        in_specs=[pl.BlockSpec((tq, LANES), lambda b, h, i: (b * nq + i, h)),
                  pl.BlockSpec((t, LANES), lambda b, h, i: (b, h)),
                  pl.BlockSpec((t, LANES), lambda b, h, i: (b, h)),
                  pl.BlockSpec((tq, cw), lambda b, h, i: (b * nq + i, 0)),
                  pl.BlockSpec((None, 1, t), lambda b, h, i: (b * nh + h, 0, 0))],
        out_specs=pl.BlockSpec((tq, LANES), lambda b, h, i: (b * nq + i, h)),
        out_shape=jax.ShapeDtypeStruct((nb * t, nh * LANES), BF16),
        compiler_params=_cparams(("parallel", "parallel", "arbitrary")),
        name="fox_attn",
    )(qn, kn, vn, c, c_t)


def _page_bias_kernel(pt_ref, *refs, n_pages, page):
    lf_refs = refs[:n_pages]
    new_ref = refs[n_pages]
    o_ref = refs[n_pages + 1]
    sel = ((_iota((page + 8, page), 0) < _iota((page + 8, page), 1)) | (_iota((page + 8, page), 0) >= page)).astype(BF16)
    carry = new_ref[...]
    for p in range(n_pages - 1, -1, -1):
        x1, x2, x3 = _split3(lf_refs[p][...])
        sums = _dot_nt(sel, x1) + _dot_nt(sel, x2) + _dot_nt(sel, x3)
        o_ref[p * page:(p + 1) * page, :] = sums[:page, :] + carry
        carry = carry + sums[page:page + 1, :]


def _page_bias(page_table, cache_logf_t, lf_new):
    nbs, n_pages = page_table.shape
    _, _, nh, page = cache_logf_t.shape
    in_specs = [pl.BlockSpec((None, None, nh, page),
                             functools.partial(lambda b, pt, p: (0, pt[b * n_pages + p], 0, 0), p=p))
                for p in range(n_pages)]
    in_specs.append(pl.BlockSpec((None, 1, nh), lambda b, pt: (b, 0, 0)))
    return pl.pallas_call(
        functools.partial(_page_bias_kernel, n_pages=n_pages, page=page),
        grid_spec=pltpu.PrefetchScalarGridSpec(
            num_scalar_prefetch=1,
            grid=(nbs,),
            in_specs=in_specs,
            out_specs=pl.BlockSpec((None, n_pages * page, nh), lambda b, pt: (b, 0, 0))),
        out_shape=jax.ShapeDtypeStruct((nbs, n_pages * page, nh), F32),
        compiler_params=_cparams(("arbitrary",)),
        name="page_bias",
    )(page_table.reshape(-1), *([cache_logf_t] * n_pages), lf_new)


def _paged_attn_kernel(pt_ref, q_ref, kn_ref, vn_ref, *refs, scale, page, nh, pps):
    k_refs, v_refs = refs[:pps], refs[pps:2 * pps]
    b_ref, o_ref, m_ref, l_ref, acc_ref = refs[2 * pps:]
    p = pl.program_id(1)
    q = q_ref[...]

    @pl.when(p == 0)
    def _():
        s_new = jnp.sum(q * kn_ref[...], axis=-1, keepdims=True) * scale
        m_ref[...] = jnp.broadcast_to(s_new, m_ref.shape)
        l_ref[...] = jnp.ones(l_ref.shape, F32)
        acc_ref[...] = vn_ref[...]

    rows = page * nh
    groups = rows // LANES
    eye = _iota((LANES, LANES), 0) == _iota((LANES, LANES), 1)
    ones = jnp.ones((2 * LANES, LANES), BF16)
    qs = q * scale
    scores = []
    for k_ref, g in zip(k_refs, range(pps)):
        kq = (k_ref[...] * qs[None]).reshape(groups, LANES, LANES)
        z = (kq + jnp.where(eye[None], b_ref[g][:, None, :], 0.0)).reshape(rows, LANES)
        hi = z.astype(BF16)
        lo = (z - hi.astype(F32)).astype(BF16)
        scores.append(_dot(jnp.concatenate([hi, lo], axis=1), ones).reshape(page, nh, LANES))
    m_old = m_ref[...]
    m_new = m_old
    for s in scores:
        m_new = jnp.maximum(m_new, jnp.max(s, axis=0))
    alpha = jnp.exp(m_old - m_new)
    l = alpha * l_ref[...]
    acc = alpha * acc_ref[...]
    for s, v_ref in zip(scores, v_refs):
        pr = jnp.exp(s - m_new[None])
        l = l + jnp.sum(pr, axis=0)
        acc = acc + jnp.sum(pr * v_ref[...], axis=0)
    l_ref[...] = l
    acc_ref[...] = acc
    m_ref[...] = m_new

    @pl.when(p == pl.num_programs(1) - 1)
    def _():
        o_ref[...] = acc_ref[...] / l_ref[...]


def _paged_attn(page_table, q, k_new, v_new, cache_k, cache_v, bias, pps):
    nbs, n_pages = page_table.shape
    _, _, page, nh, dh = cache_k.shape
    groups = page * nh // LANES
    tok = pl.BlockSpec((None, nh, dh), lambda b, p, pt: (b, 0, 0))
    pgs = [pl.BlockSpec((None, None, page, nh, dh),
                        functools.partial(lambda b, p, pt, g: (0, pt[b * n_pages + p * pps + g], 0, 0, 0), g=g))
           for g in range(pps)]
    return pl.pallas_call(
        functools.partial(_paged_attn_kernel, scale=dh ** -0.5, page=page, nh=nh, pps=pps),
        grid_spec=pltpu.PrefetchScalarGridSpec(
            num_scalar_prefetch=1,
            grid=(nbs, n_pages // pps),
            in_specs=[tok, tok, tok] + pgs + pgs +
                     [pl.BlockSpec((None, pps, groups, LANES), lambda b, p, pt: (b, p, 0, 0))],
            out_specs=tok,
            scratch_shapes=[pltpu.VMEM((nh, dh), F32)] * 3),
        out_shape=jax.ShapeDtypeStruct((nbs, nh, dh), F32),
        compiler_params=_cparams(("parallel", "arbitrary")),
        name="paged_attn",
    )(page_table.reshape(-1), q, k_new, v_new, *([cache_k] * pps), *([cache_v] * pps), bias)


def _hgrn_levels(c):
    levels = []
    m = 1
    while m < c:
        levels.append(m)
        m *= 2
    return levels


def _hgrn_prompt_kernel(hq_ref, hf_ref, hi_ref, hg_ref, lb_ref, gain_ref, o_ref, st_ref, *, t, c, hp):
    levels = _hgrn_levels(c)
    row = _iota((c, c), 0)
    col = _iota((c, c), 1)
    eye = row == col
    cums, masks = [(row >= col).astype(BF16)], []
    for m in levels:
        boundary = (row // (2 * m)) * (2 * m) + (m - 1)
        cums.append((col <= boundary).astype(BF16))
        same = (row // (2 * m)) == (col // (2 * m))
        masks.append(same & ((row % (2 * m)) >= m) & ((col % (2 * m)) < m))
    cum_all = jnp.concatenate(cums, axis=0)
    gain = gain_ref[...]

    def head_chunk(r0, hd, s_t):
        sl = slice(hd * LANES, (hd + 1) * LANES)
        lb = lb_ref[:, sl]
        hq = hq_ref[pl.ds(r0, c), sl]
        fg = lb + (1.0 - lb) * _sigmoid(hf_ref[pl.ds(r0, c), sl])
        q = hq * _sigmoid(hq)
        k = 1.0 - fg
        v = hi_ref[pl.ds(r0, c), sl].astype(BF16)
        sums = _sel_dot(cum_all, jnp.log(fg))
        b = sums[:c, :]
        a = jnp.where(eye, _dot_nt(q.astype(BF16), k.astype(BF16)), 0.0)
        for li in range(len(levels)):
            rl = sums[(li + 1) * c:(li + 2) * c, :]
            ql = q * jnp.exp(jnp.minimum(b - rl, 0.0))
            kl = k * jnp.exp(jnp.minimum(rl - b, 0.0))
            a = a + jnp.where(masks[li], _dot_nt(ql.astype(BF16), kl.astype(BF16)), 0.0)
        o = _dot(a.astype(BF16), v) + _dot_nt((q * jnp.exp(b)).astype(BF16), s_t.astype(BF16))
        b_last = b[c - 1:c, :]
        kd = k * jnp.exp(b_last - b)
        s_t = s_t * jnp.exp(b_last) + _dot_tn(v, kd.astype(BF16))
        ms = jnp.mean(o * o, axis=-1, keepdims=True)
        hg = hg_ref[pl.ds(r0, c), sl]
        o_ref[pl.ds(r0, c), sl] = (o * lax.rsqrt(ms + RMS_EPS) * gain * (hg * _sigmoid(hg))).astype(o_ref.dtype)
        return s_t

    def chunk(n, states):
        r0 = pl.multiple_of(n * c, c)
        return tuple(head_chunk(r0, hd, states[hd]) for hd in range(hp))

    states = lax.fori_loop(0, t // c, chunk, tuple(jnp.zeros((LANES, LANES), F32) for _ in range(hp)))
    ident = (_iota((LANES, LANES), 0) == _iota((LANES, LANES), 1)).astype(BF16)
    for hd in range(hp):
        s1, s2, s3 = _split3(states[hd])
        st_ref[hd] = _dot_nt(ident, s1) + _dot_nt(ident, s2) + _dot_nt(ident, s3)


def _hgrn_prompt(proj, lb, gain, nb, t, nh, blk_q, blk_f, blk_i, blk_g, c, hp):
    w = hp * LANES

    def col(blk):
        assert blk % hp == 0
        return pl.BlockSpec((t, w), lambda b, h: (b, blk // hp + h))
    return pl.pallas_call(
        functools.partial(_hgrn_prompt_kernel, t=t, c=c, hp=hp),
        grid=(nb, nh // hp),
        in_specs=[col(blk_q), col(blk_f), col(blk_i), col(blk_g),
                  pl.BlockSpec((1, w), lambda b, h: (0, h)),
                  pl.BlockSpec((1, LANES), lambda b, h: (0, 0))],
        out_specs=[pl.BlockSpec((t, w), lambda b, h: (b, h)),
                   pl.BlockSpec((None, hp, LANES, LANES), lambda b, h: (b, h, 0, 0))],
        out_shape=[jax.ShapeDtypeStruct((nb * t, nh * LANES), BF16),
                   jax.ShapeDtypeStruct((nb, nh, LANES, LANES), F32)],
        compiler_params=_cparams(("parallel", "parallel")),
        name="hgrn_prompt",
    )(proj, proj, proj, proj, lb, gain)


def _hgrn_sample_kernel(hq_ref, hf_ref, hi_ref, hg_ref, lb_ref, gain_ref, s_ref, o_ref, so_ref, *, group):
    eye = _iota((LANES, LANES), 0) == _iota((LANES, LANES), 1)

    def column(rowvec):
        return jnp.sum(jnp.where(eye, jnp.broadcast_to(rowvec, (LANES, LANES)), 0.0), axis=-1, keepdims=True)

    lb = lb_ref[...]
    gain = gain_ref[...]
    for s in range(group):
        hq = hq_ref[s:s + 1, :]
        fg = lb + (1.0 - lb) * _sigmoid(hf_ref[s:s + 1, :])
        q = hq * _sigmoid(hq)
        s_new = column(fg) * s_ref[s] + column(1.0 - fg) * hi_ref[s:s + 1, :]
        so_ref[s] = s_new
        o = jnp.sum(column(q) * s_new, axis=0, keepdims=True)
        ms = jnp.mean(o * o, axis=-1, keepdims=True)
        hg = hg_ref[s:s + 1, :]
        o_ref[s:s + 1, :] = o * lax.rsqrt(ms + RMS_EPS) * gain * (hg * _sigmoid(hg))


def _hgrn_sample(proj, state, lb, gain, row_blk0, nbs, nh, blk_q, blk_f, blk_i, blk_g, group):
    def col(blk):
        return pl.BlockSpec((group, LANES), lambda h, i: (row_blk0 + i, blk + h))
    st = pl.BlockSpec((None, group, None, LANES, LANES), lambda h, i: (0, i, h, 0, 0))
    return pl.pallas_call(
        functools.partial(_hgrn_sample_kernel, group=group),
        grid=(nh, nbs // group),
        in_specs=[col(blk_q), col(blk_f), col(blk_i), col(blk_g),
                  pl.BlockSpec((1, LANES), lambda h, i: (0, h)),
                  pl.BlockSpec((1, LANES), lambda h, i: (0, 0)),
                  st],
        out_specs=[pl.BlockSpec((group, LANES), lambda h, i: (i, h)), st],
        out_shape=[jax.ShapeDtypeStruct((nbs, nh * LANES), F32),
                   jax.ShapeDtypeStruct((1,) + state.shape[1:], F32)],
        compiler_params=_cparams(("parallel", "parallel")),
        name="hgrn_sample",
    )(proj, proj, proj, proj, lb, gain, state)


def _merge_kernel(of_ref, oh_ref, wf_ref, wh_ref, ga_ref, gb_ref, o_ref):
    a = _dot(of_ref[...], wf_ref[...])
    b = _dot(oh_ref[...], wh_ref[...])
    o_ref[...] = (_sigmoid(ga_ref[...]) * a + _sigmoid(gb_ref[...]) * b).astype(o_ref.dtype)


def _merge(o_fox, o_hg, wf, wh, proj, blk_ga, blk_gb, tm, tn):
    m, kf = o_fox.shape
    n = wf.shape[1]
    return pl.pallas_call(
        _merge_kernel,
        grid=(m // tm, n // tn),
        in_specs=[pl.BlockSpec((tm, kf), lambda i, j: (i, 0)),
                  pl.BlockSpec((tm, kf), lambda i, j: (i, 0)),
                  pl.BlockSpec((kf, tn), lambda i, j: (0, j)),
                  pl.BlockSpec((kf, tn), lambda i, j: (0, j)),
                  pl.BlockSpec((tm, tn), lambda i, j: (i, blk_ga + j)),
                  pl.BlockSpec((tm, tn), lambda i, j: (i, blk_gb + j))],
        out_specs=pl.BlockSpec((tm, tn), lambda i, j: (i, j)),
        out_shape=jax.ShapeDtypeStruct((m, n), BF16),
        compiler_params=_cparams(("parallel", "arbitrary")),
        name="branch_merge",
    )(o_fox, o_hg, wf, wh, proj, proj)


def _outproj_kernel(mg_ref, w_ref, x_ref, g_ref, wr_ref, br_ref, h_ref, hn_ref, lg_ref, hrow_ref, *, tn):
    j = pl.program_id(1)
    hblk = x_ref[...] + _dot(mg_ref[...], w_ref[...])
    h_ref[...] = hblk
    c0 = pl.multiple_of(j * tn, tn)
    hrow_ref[:, pl.ds(c0, tn)] = hblk

    @pl.when(j == pl.num_programs(1) - 1)
    def _():
        h = hrow_ref[...]
        ms = jnp.mean(h * h, axis=-1, keepdims=True)
        hn = (h * lax.rsqrt(ms + RMS_EPS) * g_ref[...]).astype(BF16)
        lg_ref[...] = _dot(hn, wr_ref[...]) + br_ref[...]
        half = hn.shape[1] // 2
        lo = lax.bitcast_convert_type(hn[:, :half].astype(F32), jnp.uint32) >> 16
        hi = lax.bitcast_convert_type(hn[:, half:].astype(F32), jnp.uint32)
        hn_ref[...] = hi | lo


def _outproj(merged, w_out, x, gain, w_router, b_router, tm, tn):
    m, d = x.shape
    ne = w_router.shape[1]
    return pl.pallas_call(
        functools.partial(_outproj_kernel, tn=tn),
        grid=(m // tm, d // tn),
        in_specs=[pl.BlockSpec((tm, d), lambda i, j: (i, 0)),
                  pl.BlockSpec((d, tn), lambda i, j: (0, j)),
                  pl.BlockSpec((tm, tn), lambda i, j: (i, j)),
                  pl.BlockSpec((1, d), lambda i, j: (0, 0)),
                  pl.BlockSpec((d, ne), lambda i, j: (0, 0)),
                  pl.BlockSpec((1, ne), lambda i, j: (0, 0))],
        out_specs=[pl.BlockSpec((tm, tn), lambda i, j: (i, j)),
                   pl.BlockSpec((tm, d // 2), lambda i, j: (i, 0)),
                   pl.BlockSpec((tm, ne), lambda i, j: (i, 0))],
        out_shape=[jax.ShapeDtypeStruct((m, d), F32),
                   jax.ShapeDtypeStruct((m, d // 2), jnp.uint32),
                   jax.ShapeDtypeStruct((m, ne), F32)],
        scratch_shapes=[pltpu.VMEM((tm, d), F32)],
        compiler_params=_cparams(("parallel", "arbitrary")),
        name="out_proj",
    )(merged, w_out, x, gain, w_router, b_router)


def _gather_kernel(tok_ref, va_ref, x_hbm, o_ref, buf, sem, *, rows):
    i = pl.program_id(0)
    last = pl.num_programs(0) - 1

    def issue(blk, slot):
        def start(r, carry):
            t = tok_ref[blk * rows + r]
            pltpu.make_async_copy(x_hbm.at[pl.ds(t, 1), :], buf.at[slot, pl.ds(r, 1), :], sem.at[slot]).start()
            return carry
        lax.fori_loop(0, rows, start, 0, unroll=8)

    @pl.when(jnp.logical_and(i == 0, va_ref[0] == 1))
    def _():
        issue(0, 0)

    nxt = jnp.minimum(i + 1, last)

    @pl.when(jnp.logical_and(i < last, va_ref[nxt] == 1))
    def _():
        issue(nxt, nxt % 2)

    slot = i % 2

    @pl.when(va_ref[i] == 1)
    def _():
        pltpu.make_async_copy(x_hbm.at[pl.ds(0, rows), :], buf.at[slot], sem.at[slot]).wait()
        u = buf[slot]
        half = u.shape[1]
        o_ref[:, :half] = lax.bitcast_convert_type(u << 16, F32).astype(o_ref.dtype)
        o_ref[:, half:] = lax.bitcast_convert_type(u & jnp.uint32(0xFFFF0000), F32).astype(o_ref.dtype)

    @pl.when(va_ref[i] == 0)
    def _():
        o_ref[...] = jnp.zeros(o_ref.shape, o_ref.dtype)


def _gather_rows(row_token, valid, x, rows):
    n_rows = row_token.shape[0]
    dw = x.shape[1]
    d = 2 * dw
    assert x.shape[0] >= rows
    return pl.pallas_call(
        functools.partial(_gather_kernel, rows=rows),
        grid_spec=pltpu.PrefetchScalarGridSpec(
            num_scalar_prefetch=2,
            grid=(n_rows // rows,),
            in_specs=[pl.BlockSpec(memory_space=pl.ANY)],
            out_specs=pl.BlockSpec((rows, d), lambda i, tok, va: (i, 0)),
            scratch_shapes=[pltpu.VMEM((2, rows, dw), x.dtype), pltpu.SemaphoreType.DMA((2,))]),
        out_shape=jax.ShapeDtypeStruct((n_rows, d), BF16),
        compiler_params=_cparams(("arbitrary",)),
        name="moe_gather",
    )(row_token, valid, x)


def _weight_copy(w_hbm, e, col, width, stage, sem):
    return pltpu.make_async_copy(w_hbm.at[e, :, pl.ds(pl.multiple_of(col, LANES), width)], stage, sem)


def _stream_weights(ne_ref, nx_ref, lr_ref, be_ref, ri_ref, copies):
    j = pl.program_id(0)
    i = pl.program_id(1)
    slot = (j * ri_ref[ri_ref.shape[0] - 1] + ri_ref[i]) % 2

    @pl.when(ne_ref[i] == 1)
    def _():
        cur = copies(be_ref[i], j, slot)

        @pl.when(jnp.logical_and(j == 0, i == 0))
        def _():
            for c in cur:
                c.start()

        for c in cur:
            c.wait()
        jn = jnp.where(lr_ref[i] == 1, j + 1, j)

        @pl.when(jn < pl.num_programs(0))
        def _():
            for c in copies(nx_ref[i], jn, 1 - slot):
                c.start()

    return slot


def _moe_up_kernel(be_ref, ne_ref, va_ref, nx_ref, lr_ref, ri_ref, x_ref, w_hbm, bg_ref, bu_ref, o_ref,
                   sg, su, sem, *, tf, f):
    i = pl.program_id(1)

    def copies(e, jj, slot):
        return (_weight_copy(w_hbm, e, jj * tf, tf, sg.at[slot], sem.at[0, slot]),
                _weight_copy(w_hbm, e, f + jj * tf, tf, su.at[slot], sem.at[1, slot]))

    slot = _stream_weights(ne_ref, nx_ref, lr_ref, be_ref, ri_ref, copies)

    @pl.when(va_ref[i] == 1)
    def _():
        x = x_ref[...]
        g = jnp.minimum(_dot(x, sg[slot].astype(BF16)) + bg_ref[...], SWIGLU_LIMIT)
        u = jnp.clip(_dot(x, su[slot].astype(BF16)) + bu_ref[...], -SWIGLU_LIMIT, SWIGLU_LIMIT)
        o_ref[...] = ((u + 1.0) * g * _sigmoid(SWIGLU_ALPHA * g)).astype(o_ref.dtype)

    @pl.when(va_ref[i] == 0)
    def _():
        o_ref[...] = jnp.zeros(o_ref.shape, o_ref.dtype)


def _moe_up(sched, xs, w_gate_up, b_gate_up, rows, tf):
    n_rows, d = xs.shape
    f = w_gate_up.shape[2] // 2
    nf = f // tf
    return pl.pallas_call(
        functools.partial(_moe_up_kernel, tf=tf, f=f),
        grid_spec=pltpu.PrefetchScalarGridSpec(
            num_scalar_prefetch=6,
            grid=(nf, n_rows // rows),
            in_specs=[pl.BlockSpec((rows, d), lambda j, i, be, *_: (i, 0)),
                      pl.BlockSpec(memory_space=pl.ANY),
                      pl.BlockSpec((None, 1, tf), lambda j, i, be, *_: (be[i], 0, j)),
                      pl.BlockSpec((None, 1, tf), lambda j, i, be, *_: (be[i], 0, nf + j))],
            out_specs=pl.BlockSpec((rows, tf), lambda j, i, be, *_: (i, j)),
            scratch_shapes=[pltpu.VMEM((2, d, tf), F32), pltpu.VMEM((2, d, tf), F32),
                            pltpu.SemaphoreType.DMA((2, 2))]),
        out_shape=jax.ShapeDtypeStruct((n_rows, f), BF16),
        compiler_params=_cparams(("arbitrary", "arbitrary")),
        name="moe_up",
    )(*sched, xs, w_gate_up, b_gate_up, b_gate_up)


def _moe_down_kernel(be_ref, ne_ref, va_ref, nx_ref, lr_ref, ri_ref, a_ref, w_hbm, b_ref, o_ref, sw, sem, *, tn):
    i = pl.program_id(1)

    def copies(e, jj, slot):
        return (_weight_copy(w_hbm, e, jj * tn, tn, sw.at[slot], sem.at[slot]),)

    slot = _stream_weights(ne_ref, nx_ref, lr_ref, be_ref, ri_ref, copies)

    @pl.when(va_ref[i] == 1)
    def _():
        o_ref[...] = _dot(a_ref[...], sw[slot].astype(BF16)) + b_ref[...]

    @pl.when(va_ref[i] == 0)
    def _():
        o_ref[...] = jnp.zeros(o_ref.shape, o_ref.dtype)


def _moe_down(sched, act, w_down, b_down, rows, tn):
    n_rows, f = act.shape
    d = w_down.shape[2]
    return pl.pallas_call(
        functools.partial(_moe_down_kernel, tn=tn),
        grid_spec=pltpu.PrefetchScalarGridSpec(
            num_scalar_prefetch=6,
            grid=(d // tn, n_rows // rows),
            in_specs=[pl.BlockSpec((rows, f), lambda j, i, be, *_: (i, 0)),
                      pl.BlockSpec(memory_space=pl.ANY),
                      pl.BlockSpec((None, 1, tn), lambda j, i, be, *_: (be[i], 0, j))],
            out_specs=pl.BlockSpec((rows, tn), lambda j, i, be, *_: (i, j)),
            scratch_shapes=[pltpu.VMEM((2, f, tn), F32), pltpu.SemaphoreType.DMA((2,))]),
        out_shape=jax.ShapeDtypeStruct((n_rows, d), F32),
        compiler_params=_cparams(("arbitrary", "arbitrary")),
        name="moe_down",
    )(*sched, act, w_down, b_down)


def _combine_kernel(pos_ref, h_ref, gt_ref, g_ref, y_hbm, h2_ref, hn_ref, buf, sem, *, tm, top_k):
    i = pl.program_id(0)
    last = pl.num_programs(0) - 1
    n = tm * top_k

    def issue(blk, slot):
        def start(a, carry):
            src = pos_ref[blk * n + a]
            pltpu.make_async_copy(y_hbm.at[pl.ds(src, 1), :],
                                  buf.at[slot, a % top_k, pl.ds(a // top_k, 1), :], sem.at[slot]).start()
            return carry
        lax.fori_loop(0, n, start, 0, unroll=8)

    @pl.when(i == 0)
    def _():
        issue(0, 0)

    nxt = jnp.minimum(i + 1, last)

    @pl.when(i < last)
    def _():
        issue(nxt, nxt % 2)

    slot = i % 2
    for k in range(top_k):
        pltpu.make_async_copy(y_hbm.at[pl.ds(0, tm), :], buf.at[slot, k], sem.at[slot]).wait()
    h = h_ref[...]
    gt = gt_ref[...]
    for k in range(top_k):
        h = h + gt[:, k:k + 1] * buf[slot, k]
    h2_ref[...] = h
    ms = jnp.mean(h * h, axis=-1, keepdims=True)
    hn_ref[...] = (h * lax.rsqrt(ms + RMS_EPS) * g_ref[...]).astype(hn_ref.dtype)


def _combine(pos, h, gates, gain, y_rows, tm):
    m, d = h.shape
    assert y_rows.shape[0] >= tm
    return pl.pallas_call(
        functools.partial(_combine_kernel, tm=tm, top_k=TOP_K),
        grid_spec=pltpu.PrefetchScalarGridSpec(
            num_scalar_prefetch=1,
            grid=(m // tm,),
            in_specs=[pl.BlockSpec((tm, d), lambda i, pos: (i, 0)),
                      pl.BlockSpec((tm, TOP_K), lambda i, pos: (i, 0)),
                      pl.BlockSpec((1, d), lambda i, pos: (0, 0)),
                      pl.BlockSpec(memory_space=pl.ANY)],
            out_specs=[pl.BlockSpec((tm, d), lambda i, pos: (i, 0)),
                       pl.BlockSpec((tm, d), lambda i, pos: (i, 0))],
            scratch_shapes=[pltpu.VMEM((2, TOP_K, tm, d), F32), pltpu.SemaphoreType.DMA((2,))]),
        out_shape=[jax.ShapeDtypeStruct((m, d), F32), jax.ShapeDtypeStruct((m, d), BF16)],
        compiler_params=_cparams(("arbitrary",)),
        name="moe_combine",
    )(pos, h, gates, gain, y_rows)


def _ple_kernel(hn_ref, wg_ref, p_ref, wp_ref, h_ref, o_ref):
    gate = _sigmoid(_dot(hn_ref[...], wg_ref[...]))
    o_ref[...] = h_ref[...] + gate * _dot(p_ref[...], wp_ref[...])


def _ple(hn, w_gate, p, w_proj, h, row0, n, tm, tn):
    d = h.shape[1]
    pd = p.shape[1]
    r0 = row0 // tm
    assert row0 % tm == 0 and n % tm == 0
    return pl.pallas_call(
        _ple_kernel,
        grid=(n // tm, d // tn),
        in_specs=[pl.BlockSpec((tm, d), lambda i, j: (r0 + i, 0)),
                  pl.BlockSpec((d, tn), lambda i, j: (0, j)),
                  pl.BlockSpec((tm, pd), lambda i, j: (i, 0)),
                  pl.BlockSpec((pd, tn), lambda i, j: (0, j)),
                  pl.BlockSpec((tm, tn), lambda i, j: (r0 + i, j))],
        out_specs=pl.BlockSpec((tm, tn), lambda i, j: (i, j)),
        out_shape=jax.ShapeDtypeStruct((n, d), F32),
        compiler_params=_cparams(("parallel", "arbitrary")),
        name="ple",
    )(hn, w_gate, p, w_proj, h)


def _route(logits, n_experts, rows):
    n = logits.shape[0]
    n_assign = n * TOP_K
    n_blocks = (n_assign + n_experts * (rows - 1) + rows - 1) // rows
    n_rows = n_blocks * rows
    top_logits, top_idx = lax.top_k(logits, TOP_K)
    gates = jax.nn.softmax(top_logits, axis=-1)
    expert = top_idx.reshape(-1).astype(jnp.int32)
    token = jnp.arange(n_assign, dtype=jnp.int32) // TOP_K
    onehot = expert[:, None] == jnp.arange(n_experts, dtype=jnp.int32)[None, :]
    cb = _divisor(n_assign, 256, 8)
    oh = onehot.reshape(n_assign // cb, cb, n_experts).astype(BF16)
    tri = (jnp.arange(cb)[:, None] >= jnp.arange(cb)[None, :]).astype(BF16)
    within = jnp.einsum('ij,bjk->bik', tri, oh, preferred_element_type=F32)
    totals = within[:, -1, :]
    offsets = jnp.cumsum(totals, axis=0) - totals
    csum = (within + offsets[:, None, :]).reshape(n_assign, n_experts)
    rank = jnp.sum(jnp.where(onehot, csum, 0.0), axis=1).astype(jnp.int32) - 1
    counts = (offsets[-1] + totals[-1]).astype(jnp.int32)
    padded = (counts + rows - 1) // rows * rows
    pend = jnp.cumsum(padded)
    pos = (pend - padded)[expert] + rank
    row_token = jnp.zeros((n_rows,), jnp.int32).at[pos].set(token)
    blk_start = jnp.arange(n_blocks, dtype=jnp.int32) * rows
    block_expert = jnp.minimum(jnp.sum((blk_start[:, None] >= pend[None, :]).astype(jnp.int32), axis=1), n_experts - 1)
    valid = (blk_start < pend[-1]).astype(jnp.int32)
    last_valid = block_expert[jnp.maximum(pend[-1] // rows - 1, 0)]
    block_expert = jnp.where(valid == 1, block_expert, last_valid)
    prev = jnp.concatenate([jnp.full((1,), -1, jnp.int32), block_expert[:-1]])
    new_expert = (block_expert != prev).astype(jnp.int32)
    run_index = jnp.cumsum(new_expert) - 1
    run_info = jnp.concatenate([run_index, run_index[-1:] + 1]).astype(jnp.int32)
    after = jnp.sum((block_expert[None, :] <= block_expert[:, None]).astype(jnp.int32), axis=1)
    last_run = (after >= n_blocks).astype(jnp.int32)
    next_expert = block_expert[jnp.where(after >= n_blocks, 0, after)]
    sched = (block_expert, new_expert, valid, next_expert, last_run, run_info)
    return pos.astype(jnp.int32), gates, row_token, sched


def kernel(x_prompt, x_sample, cache_k, cache_v, cache_logf, state_hgrn, page_table, p_prompt, p_sample,
           attn_norm, w_in, fox_fgate_bias, q_norm, k_norm, hg_lower_bound, hg_out_norm, w_proj_fox,
           w_proj_hg, w_out, ffn_norm, w_router, b_router, w_gate_up, b_gate_up, w_down, b_down,
           ple_norm, w_ple_gate, w_ple_proj):
    nb, t, d = x_prompt.shape
    nbs = x_sample.shape[0]
    assert x_sample.shape[1] == 1 and cache_k.shape[0] == 1
    _, _, page, nh, dh = cache_k.shape
    assert dh == LANES
    fw = nh * dh
    nhg = state_hgrn.shape[2]
    hw = nhg * LANES
    assert state_hgrn.shape[3] == LANES and state_hgrn.shape[4] == LANES and hw == fw
    n_experts = w_router.shape[2]
    mp = nb * t
    m = mp + nbs

    tm = _divisor(m, 704, 16)
    tn = _divisor(fw, 512, LANES)
    tq = _divisor(t, 512, LANES)
    chunk = LANES
    group = 8
    ts = _divisor(nbs, 128, 16)
    tp = _divisor(mp, 512, 16)
    hp = 4 if nhg % 4 == 0 and (3 * fw // LANES) % 4 == 0 else (2 if nhg % 2 == 0 else 1)
    pps = _divisor(page_table.shape[1], 4, 1)
    moe_rows = 256 if m * TOP_K // n_experts >= 512 else 16
    tm_out = tm
    tf = _divisor(d, 512, LANES)
    tdn = _divisor(d, 1024, LANES)
    assert nbs % group == 0 and mp % group == 0 and t % chunk == 0

    x = jnp.concatenate([x_prompt.reshape(mp, d), x_sample.reshape(nbs, d)], axis=0)
    w = w_in[0]
    ff0 = 3 * fw
    w_al = jnp.concatenate([w[:, :ff0], w[:, ff0 + nh:],
                            jnp.pad(w[:, ff0:ff0 + nh], ((0, 0), (0, tn - nh)))], axis=1).astype(BF16)
    blk = lambda col: col // LANES
    c_hq, c_hf, c_hi, c_hg = ff0, ff0 + hw, ff0 + 2 * hw, ff0 + 3 * hw
    c_ga = ff0 + 4 * hw
    c_gb = c_ga + d
    c_ff = c_gb + d
    lb = jax.nn.softmax(hg_lower_bound.astype(F32), axis=0)[0:1]
    fbias = jnp.pad(fox_fgate_bias, ((0, 0), (0, LANES - nh)))

    proj = _inproj(x, attn_norm, w_al, tm, tn)
    qn_p, kn_p, vn_p, logf_p = _prep(proj, q_norm, k_norm, fbias, 0, mp, tp, fw, blk(c_ff), LANES)
    qn_s, kn_s, vn_s, logf_s = _prep(proj, q_norm, k_norm, fbias, mp, nbs, ts, fw, blk(c_ff), LANES)

    c = _seq_cumsum(logf_p, nb, t)
    c_t = c[:, :nh].reshape(nb, t, nh).transpose(0, 2, 1).reshape(nb * nh, 1, t)
    o_fox_p = _fox_attn(qn_p, kn_p, vn_p, c, c_t, nb, t, nh, tq)

    lf_new = logf_s[:, :nh].reshape(nbs, 1, nh)
    bias = _page_bias(page_table, jnp.swapaxes(cache_logf, 2, 3), lf_new)
    bias = bias.reshape(nbs, page_table.shape[1], page * nh // LANES, LANES)
    o_fox_s = _paged_attn(page_table, qn_s.reshape(nbs, nh, dh), kn_s.reshape(nbs, nh, dh),
                          vn_s.reshape(nbs, nh, dh), cache_k, cache_v, bias, pps)

    o_hg_p, st_p = _hgrn_prompt(proj, lb, hg_out_norm, nb, t, nhg, blk(c_hq), blk(c_hf), blk(c_hi), blk(c_hg),
                                chunk, hp)
    o_hg_s, st_s = _hgrn_sample(proj, state_hgrn, lb, hg_out_norm, mp // group, nbs, nhg,
                                blk(c_hq), blk(c_hf), blk(c_hi), blk(c_hg), group)

    o_fox = jnp.concatenate([o_fox_p, o_fox_s.reshape(nbs, fw).astype(BF16)], axis=0)
    o_hg = jnp.concatenate([o_hg_p, o_hg_s.astype(BF16)], axis=0)
    merged = _merge(o_fox, o_hg, w_proj_fox[0].astype(BF16), w_proj_hg[0].astype(BF16), proj,
                    c_ga // tn, c_gb // tn, tm, tn)

    ne_pad = -(-n_experts // LANES) * LANES
    w_r = jnp.pad(w_router[0], ((0, 0), (0, ne_pad - n_experts))).astype(BF16)
    b_r = jnp.pad(b_router, ((0, 0), (0, ne_pad - n_experts)))
    h, hn, logits = _outproj(merged, w_out[0].astype(BF16), x, ffn_norm, w_r, b_r, tm_out, tn)

    pos, gates, row_token, sched = _route(logits[:, :n_experts], n_experts, moe_rows)
    xs = _gather_rows(row_token, sched[2], hn, moe_rows)
    act = _moe_up(sched, xs, w_gate_up.reshape(w_gate_up.shape[1:]), b_gate_up.reshape(n_experts, 1, -1),
                  moe_rows, tf)
    y_rows = _moe_down(sched, act, w_down.reshape(w_down.shape[1:]), b_down.reshape(n_experts, 1, -1),
                       moe_rows, tdn)
    h2, hn2 = _combine(pos, h, gates, ple_norm, y_rows, _divisor(m, 128, 16))

    w_pg = w_ple_gate[0].astype(BF16)
    w_pp = w_ple_proj[0].astype(BF16)
    y_p = _ple(hn2, w_pg, p_prompt[0].reshape(mp, -1).astype(BF16), w_pp, h2, 0, mp, tp, tn)
    y_s = _ple(hn2, w_pg, p_sample[0].reshape(nbs, -1).astype(BF16), w_pp, h2, mp, nbs, ts, tn)

    return (y_p.reshape(nb, t, d), y_s.reshape(nbs, 1, d),
            kn_p.reshape(1, nb, t, nh, dh), vn_p.reshape(1, nb, t, nh, dh), logf_p[:, :nh].reshape(1, nb, t, nh),
            st_p[None],
            kn_s.reshape(1, nbs, 1, nh, dh), vn_s.reshape(1, nbs, 1, nh, dh), logf_s[:, :nh].reshape(1, nbs, 1, nh),
            st_s)
```

```python
import functools

import jax
import jax.numpy as jnp
from jax import lax
from jax.experimental import pallas as pl
from jax.experimental.pallas import tpu as pltpu

F32 = jnp.float32
BF16 = jnp.bfloat16

TOP_K = 4
SWIGLU_LIMIT = 7.0
SWIGLU_ALPHA = 1.702
RMS_EPS = 1e-6
LANES = 128
MASK_VALUE = -1e30
VMEM_LIMIT = 56 * 1024 * 1024


def _cparams(sem):
    return pltpu.CompilerParams(dimension_semantics=sem, vmem_limit_bytes=VMEM_LIMIT)


def _divisor(n, cap, mult):
    best = None
    for d in range(mult, min(n, cap) + 1, mult):
        if n % d == 0:
            best = d
    assert best is not None, (n, cap, mult)
    return best


def _dot(a, b):
    return jnp.dot(a, b, preferred_element_type=F32)


def _dot_nt(a, b):
    return lax.dot_general(a, b, (((1,), (1,)), ((), ())), preferred_element_type=F32)


def _dot_tn(a, b):
    return lax.dot_general(a, b, (((0,), (0,)), ((), ())), preferred_element_type=F32)


def _split3(x):
    x1 = x.astype(BF16)
    r1 = x - x1.astype(F32)
    x2 = r1.astype(BF16)
    x3 = (r1 - x2.astype(F32)).astype(BF16)
    return x1, x2, x3


def _sel_dot(sel, x):
    x1, x2, x3 = _split3(x)
    return _dot(sel, x1) + _dot(sel, x2) + _dot(sel, x3)


def _sigmoid(x):
    return 1.0 / (1.0 + jnp.exp(-x))


def _iota(shape, dim):
    return lax.broadcasted_iota(jnp.int32, shape, dim)


def _inproj_kernel(x_ref, g_ref, w_ref, o_ref, xn_ref):
    @pl.when(pl.program_id(1) == 0)
    def _():
        x = x_ref[...]
        ms = jnp.mean(x * x, axis=-1, keepdims=True)
        xn_ref[...] = (x * lax.rsqrt(ms + RMS_EPS) * g_ref[...]).astype(BF16)

    o_ref[...] = _dot(xn_ref[...], w_ref[...])


def _inproj(x, gain, w, tm, tn):
    m, d = x.shape
    n = w.shape[1]
    return pl.pallas_call(
        _inproj_kernel,
        grid=(m // tm, n // tn),
        in_specs=[pl.BlockSpec((tm, d), lambda i, j: (i, 0)),
                  pl.BlockSpec((1, d), lambda i, j: (0, 0)),
                  pl.BlockSpec((d, tn), lambda i, j: (0, j))],
        out_specs=pl.BlockSpec((tm, tn), lambda i, j: (i, j)),
        out_shape=jax.ShapeDtypeStruct((m, n), F32),
        scratch_shapes=[pltpu.VMEM((tm, d), BF16)],
        compiler_params=_cparams(("parallel", "arbitrary")),
        name="inproj",
    )(x, gain, w)


def _prep_kernel(fq_ref, fk_ref, fv_ref, ff_ref, qg_ref, kg_ref, fb_ref, q_ref, k_ref, v_ref, lf_ref, *, nh):
    qg = qg_ref[...]
    kg = kg_ref[...]
    for h in range(nh):
        sl = slice(h * LANES, (h + 1) * LANES)
        for src, gain, dst in ((fq_ref, qg, q_ref), (fk_ref, kg, k_ref)):
            xh = src[:, sl]
            ms = jnp.mean(xh * xh, axis=-1, keepdims=True)
            dst[:, sl] = xh * lax.rsqrt(ms + RMS_EPS) * gain
    v_ref[...] = fv_ref[...]
    z = ff_ref[...] + fb_ref[...]
    lf_ref[...] = jnp.minimum(z, 0.0) - jnp.log1p(jnp.exp(-jnp.abs(z)))


def _prep(proj, q_gain, k_gain, fbias, row0, n, tm, fw, ff_blk, ffw):
    nh = fw // LANES
    r0 = row0 // tm
    assert row0 % tm == 0 and n % tm == 0
    return pl.pallas_call(
        functools.partial(_prep_kernel, nh=nh),
        grid=(n // tm,),
        in_specs=[pl.BlockSpec((tm, fw), lambda i: (r0 + i, 0)),
                  pl.BlockSpec((tm, fw), lambda i: (r0 + i, 1)),
                  pl.BlockSpec((tm, fw), lambda i: (r0 + i, 2)),
                  pl.BlockSpec((tm, ffw), lambda i: (r0 + i, ff_blk)),
                  pl.BlockSpec((1, LANES), lambda i: (0, 0)),
                  pl.BlockSpec((1, LANES), lambda i: (0, 0)),
                  pl.BlockSpec((1, ffw), lambda i: (0, 0))],
        out_specs=[pl.BlockSpec((tm, fw), lambda i: (i, 0)),
                   pl.BlockSpec((tm, fw), lambda i: (i, 0)),
                   pl.BlockSpec((tm, fw), lambda i: (i, 0)),
                   pl.BlockSpec((tm, ffw), lambda i: (i, 0))],
        out_shape=[jax.ShapeDtypeStruct((n, fw), F32),
                   jax.ShapeDtypeStruct((n, fw), F32),
                   jax.ShapeDtypeStruct((n, fw), F32),
                   jax.ShapeDtypeStruct((n, ffw), F32)],
        compiler_params=_cparams(("parallel",)),
        name="fox_prep",
    )(proj, proj, proj, proj, q_gain, k_gain, fbias)


def _cumsum_kernel(x_ref, o_ref, *, t, c):
    tri = (_iota((c, c), 0) >= _iota((c, c), 1)).astype(BF16)
    carry = jnp.zeros((1, x_ref.shape[1]), F32)
    for n in range(t // c):
        blk = _sel_dot(tri, x_ref[n * c:(n + 1) * c, :]) + carry
        o_ref[n * c:(n + 1) * c, :] = blk
        carry = blk[c - 1:c, :]


def _seq_cumsum(x, nb, t):
    w = x.shape[1]
    return pl.pallas_call(
        functools.partial(_cumsum_kernel, t=t, c=LANES),
        grid=(nb,),
        in_specs=[pl.BlockSpec((t, w), lambda b: (b, 0))],
        out_specs=pl.BlockSpec((t, w), lambda b: (b, 0)),
        out_shape=jax.ShapeDtypeStruct((nb * t, w), F32),
        compiler_params=_cparams(("parallel",)),
        name="logf_cumsum",
    )(x)


def _fox_attn_kernel(q_ref, k_ref, v_ref, cc_ref, cr_ref, o_ref, *, tq, scale):
    h = pl.program_id(1)
    i = pl.program_id(2)
    q = q_ref[...].astype(BF16)
    lane = _iota(cc_ref.shape, 1)
    cq = jnp.sum(jnp.where(lane == h, cc_ref[...], 0.0), axis=-1, keepdims=True)
    causal = _iota((tq, tq), 0) >= _iota((tq, tq), 1)

    def block(j, carry, diagonal):
        m, l, acc = carry
        r0 = pl.multiple_of(j * tq, tq)
        ks = k_ref[pl.ds(r0, tq), :].astype(BF16)
        vs = v_ref[pl.ds(r0, tq), :].astype(BF16)
        ck = cr_ref[:, pl.ds(r0, tq)]
        s = _dot_nt(q, ks) * scale + (cq - ck)
        if diagonal:
            s = jnp.where(causal, s, MASK_VALUE)
        m_new = jnp.maximum(m, jnp.max(s, axis=-1, keepdims=True))
        alpha = jnp.exp(m - m_new)
        p = jnp.exp(s - m_new)
        l = alpha * l + jnp.sum(p, axis=-1, keepdims=True)
        acc = alpha * acc + _dot(p.astype(BF16), vs)
        return m_new, l, acc

    init = (jnp.full((tq, 1), MASK_VALUE, F32), jnp.zeros((tq, 1), F32), jnp.zeros((tq, LANES), F32))
    carry = lax.fori_loop(0, i, lambda j, c: block(j, c, False), init)
    _, l, acc = block(i, carry, True)
    o_ref[...] = (acc / l).astype(o_ref.dtype)


def _fox_attn(qn, kn, vn, c, c_t, nb, t, nh, tq):
    nq = t // tq
    cw = c.shape[1]
    return pl.pallas_call(
        functools.partial(_fox_attn_kernel, tq=tq, scale=LANES ** -0.5),
        grid=(nb, nh, nq),
        in_specs=[pl.BlockSpec((tq, LANES), lambda b, h, i: (b * nq + i, h)),
                  pl.BlockSpec((t, LANES), lambda b, h, i: (b, h)),
                  pl.BlockSpec((t, LANES), lambda b, h, i: (b, h)),
                  pl.BlockSpec((tq, cw), lambda b, h, i: (b * nq + i, 0)),
                  pl.BlockSpec((None, 1, t), lambda b, h, i: (b * nh + h, 0, 0))],
        out_specs=pl.BlockSpec((tq, LANES), lambda b, h, i: (b * nq + i, h)),
        out_shape=jax.ShapeDtypeStruct((nb * t, nh * LANES), BF16),
        compiler_params=_cparams(("parallel", "parallel", "arbitrary")),
        name="fox_attn",
    )(qn, kn, vn, c, c_t)


def _page_bias_kernel(pt_ref, *refs, n_pages, page):
    lf_refs = refs[:n_pages]
    new_ref = refs[n_pages]
    o_ref = refs[n_pages + 1]
    sel = ((_iota((page + 8, page), 0) < _iota((page + 8, page), 1)) | (_iota((page + 8, page), 0) >= page)).astype(BF16)
    carry = new_ref[...]
    for p in range(n_pages - 1, -1, -1):
        x1, x2, x3 = _split3(lf_refs[p][...])
        sums = _dot_nt(sel, x1) + _dot_nt(sel, x2) + _dot_nt(sel, x3)
        o_ref[p * page:(p + 1) * page, :] = sums[:page, :] + carry
        carry = carry + sums[page:page + 1, :]


def _page_bias(page_table, cache_logf_t, lf_new):
    nbs, n_pages = page_table.shape
    _, _, nh, page = cache_logf_t.shape
    in_specs = [pl.BlockSpec((None, None, nh, page),
                             functools.partial(lambda b, pt, p: (0, pt[b * n_pages + p], 0, 0), p=p))
                for p in range(n_pages)]
    in_specs.append(pl.BlockSpec((None, 1, nh), lambda b, pt: (b, 0, 0)))
    return pl.pallas_call(
        functools.partial(_page_bias_kernel, n_pages=n_pages, page=page),
        grid_spec=pltpu.PrefetchScalarGridSpec(
            num_scalar_prefetch=1,
            grid=(nbs,),
            in_specs=in_specs,
            out_specs=pl.BlockSpec((None, n_pages * page, nh), lambda b, pt: (b, 0, 0))),
        out_shape=jax.ShapeDtypeStruct((nbs, n_pages * page, nh), F32),
        compiler_params=_cparams(("arbitrary",)),
        name="page_bias",
    )(page_table.reshape(-1), *([cache_logf_t] * n_pages), lf_new)


def _paged_attn_kernel(pt_ref, q_ref, kn_ref, vn_ref, *refs, scale, page, nh, pps):
    k_refs, v_refs = refs[:pps], refs[pps:2 * pps]
    b_ref, o_ref, m_ref, l_ref, acc_ref = refs[2 * pps:]
    p = pl.program_id(1)
    q = q_ref[...]

    @pl.when(p == 0)
    def _():
        s_new = jnp.sum(q * kn_ref[...], axis=-1, keepdims=True) * scale
        m_ref[...] = jnp.broadcast_to(s_new, m_ref.shape)
        l_ref[...] = jnp.ones(l_ref.shape, F32)
        acc_ref[...] = vn_ref[...]

    rows = page * nh
    groups = rows // LANES
    eye = _iota((LANES, LANES), 0) == _iota((LANES, LANES), 1)
    ones = jnp.ones((2 * LANES, LANES), BF16)
    qs = q * scale
    scores = []
    for k_ref, g in zip(k_refs, range(pps)):
        kq = (k_ref[...] * qs[None]).reshape(groups, LANES, LANES)
        z = (kq + jnp.where(eye[None], b_ref[g][:, None, :], 0.0)).reshape(rows, LANES)
        hi = z.astype(BF16)
        lo = (z - hi.astype(F32)).astype(BF16)
        scores.append(_dot(jnp.concatenate([hi, lo], axis=1), ones).reshape(page, nh, LANES))
    m_old = m_ref[...]
    m_new = m_old
    for s in scores:
        m_new = jnp.maximum(m_new, jnp.max(s, axis=0))
    alpha = jnp.exp(m_old - m_new)
    l = alpha * l_ref[...]
    acc = alpha * acc_ref[...]
    for s, v_ref in zip(scores, v_refs):
        pr = jnp.exp(s - m_new[None])
        l = l + jnp.sum(pr, axis=0)
        acc = acc + jnp.sum(pr * v_ref[...], axis=0)
    l_ref[...] = l
    acc_ref[...] = acc
    m_ref[...] = m_new

    @pl.when(p == pl.num_programs(1) - 1)
    def _():
        o_ref[...] = acc_ref[...] / l_ref[...]


def _paged_attn(page_table, q, k_new, v_new, cache_k, cache_v, bias, pps):
    nbs, n_pages = page_table.shape
    _, _, page, nh, dh = cache_k.shape
    groups = page * nh // LANES
    tok = pl.BlockSpec((None, nh, dh), lambda b, p, pt: (b, 0, 0))
    pgs = [pl.BlockSpec((None, None, page, nh, dh),
                        functools.partial(lambda b, p, pt, g: (0, pt[b * n_pages + p * pps + g], 0, 0, 0), g=g))
           for g in range(pps)]
    return pl.pallas_call(
        functools.partial(_paged_attn_kernel, scale=dh ** -0.5, page=page, nh=nh, pps=pps),
        grid_spec=pltpu.PrefetchScalarGridSpec(
            num_scalar_prefetch=1,
            grid=(nbs, n_pages // pps),
            in_specs=[tok, tok, tok] + pgs + pgs +
                     [pl.BlockSpec((None, pps, groups, LANES), lambda b, p, pt: (b, p, 0, 0))],
            out_specs=tok,
            scratch_shapes=[pltpu.VMEM((nh, dh), F32)] * 3),
        out_shape=jax.ShapeDtypeStruct((nbs, nh, dh), F32),
        compiler_params=_cparams(("parallel", "arbitrary")),
        name="paged_attn",
    )(page_table.reshape(-1), q, k_new, v_new, *([cache_k] * pps), *([cache_v] * pps), bias)


def _hgrn_levels(c):
    levels = []
    m = 1
    while m < c:
        levels.append(m)
        m *= 2
    return levels


def _hgrn_prompt_kernel(hq_ref, hf_ref, hi_ref, hg_ref, lb_ref, gain_ref, o_ref, st_ref, *, t, c, hp):
    levels = _hgrn_levels(c)
    row = _iota((c, c), 0)
    col = _iota((c, c), 1)
    eye = row == col
    cums, masks = [(row >= col).astype(BF16)], []
    for m in levels:
        boundary = (row // (2 * m)) * (2 * m) + (m - 1)
        cums.append((col <= boundary).astype(BF16))
        same = (row // (2 * m)) == (col // (2 * m))
        masks.append(same & ((row % (2 * m)) >= m) & ((col % (2 * m)) < m))
    cum_all = jnp.concatenate(cums, axis=0)
    gain = gain_ref[...]

    def head_chunk(r0, hd, s_t):
        sl = slice(hd * LANES, (hd + 1) * LANES)
        lb = lb_ref[:, sl]
        hq = hq_ref[pl.ds(r0, c), sl]
        fg = lb + (1.0 - lb) * _sigmoid(hf_ref[pl.ds(r0, c), sl])
        q = hq * _sigmoid(hq)
        k = 1.0 - fg
        v = hi_ref[pl.ds(r0, c), sl].astype(BF16)
        sums = _sel_dot(cum_all, jnp.log(fg))
        b = sums[:c, :]
        a = jnp.where(eye, _dot_nt(q.astype(BF16), k.astype(BF16)), 0.0)
        for li in range(len(levels)):
            rl = sums[(li + 1) * c:(li + 2) * c, :]
            ql = q * jnp.exp(jnp.minimum(b - rl, 0.0))
            kl = k * jnp.exp(jnp.minimum(rl - b, 0.0))
            a = a + jnp.where(masks[li], _dot_nt(ql.astype(BF16), kl.astype(BF16)), 0.0)
        o = _dot(a.astype(BF16), v) + _dot_nt((q * jnp.exp(b)).astype(BF16), s_t.astype(BF16))
        b_last = b[c - 1:c, :]
        kd = k * jnp.exp(b_last - b)
        s_t = s_t * jnp.exp(b_last) + _dot_tn(v, kd.astype(BF16))
        ms = jnp.mean(o * o, axis=-1, keepdims=True)
        hg = hg_ref[pl.ds(r0, c), sl]
        o_ref[pl.ds(r0, c), sl] = (o * lax.rsqrt(ms + RMS_EPS) * gain * (hg * _sigmoid(hg))).astype(o_ref.dtype)
        return s_t

    def chunk(n, states):
        r0 = pl.multiple_of(n * c, c)
        return tuple(head_chunk(r0, hd, states[hd]) for hd in range(hp))

    states = lax.fori_loop(0, t // c, chunk, tuple(jnp.zeros((LANES, LANES), F32) for _ in range(hp)))
    ident = (_iota((LANES, LANES), 0) == _iota((LANES, LANES), 1)).astype(BF16)
    for hd in range(hp):
        s1, s2, s3 = _split3(states[hd])
        st_ref[hd] = _dot_nt(ident, s1) + _dot_nt(ident, s2) + _dot_nt(ident, s3)


def _hgrn_prompt(proj, lb, gain, nb, t, nh, blk_q, blk_f, blk_i, blk_g, c, hp):
    w = hp * LANES

    def col(blk):
        assert blk % hp == 0
        return pl.BlockSpec((t, w), lambda b, h: (b, blk // hp + h))
    return pl.pallas_call(
        functools.partial(_hgrn_prompt_kernel, t=t, c=c, hp=hp),
        grid=(nb, nh // hp),
        in_specs=[col(blk_q), col(blk_f), col(blk_i), col(blk_g),
                  pl.BlockSpec((1, w), lambda b, h: (0, h)),
                  pl.BlockSpec((1, LANES), lambda b, h: (0, 0))],
        out_specs=[pl.BlockSpec((t, w), lambda b, h: (b, h)),
                   pl.BlockSpec((None, hp, LANES, LANES), lambda b, h: (b, h, 0, 0))],
        out_shape=[jax.ShapeDtypeStruct((nb * t, nh * LANES), BF16),
                   jax.ShapeDtypeStruct((nb, nh, LANES, LANES), F32)],
        compiler_params=_cparams(("parallel", "parallel")),
        name="hgrn_prompt",
    )(proj, proj, proj, proj, lb, gain)


def _hgrn_sample_kernel(hq_ref, hf_ref, hi_ref, hg_ref, lb_ref, gain_ref, s_ref, o_ref, so_ref, *, group):
    eye = _iota((LANES, LANES), 0) == _iota((LANES, LANES), 1)

    def column(rowvec):
        return jnp.sum(jnp.where(eye, jnp.broadcast_to(rowvec, (LANES, LANES)), 0.0), axis=-1, keepdims=True)

    lb = lb_ref[...]
    gain = gain_ref[...]
    for s in range(group):
        hq = hq_ref[s:s + 1, :]
        fg = lb + (1.0 - lb) * _sigmoid(hf_ref[s:s + 1, :])
        q = hq * _sigmoid(hq)
        s_new = column(fg) * s_ref[s] + column(1.0 - fg) * hi_ref[s:s + 1, :]
        so_ref[s] = s_new
        o = jnp.sum(column(q) * s_new, axis=0, keepdims=True)
        ms = jnp.mean(o * o, axis=-1, keepdims=True)
        hg = hg_ref[s:s + 1, :]
        o_ref[s:s + 1, :] = o * lax.rsqrt(ms + RMS_EPS) * gain * (hg * _sigmoid(hg))


def _hgrn_sample(proj, state, lb, gain, row_blk0, nbs, nh, blk_q, blk_f, blk_i, blk_g, group):
    def col(blk):
        return pl.BlockSpec((group, LANES), lambda h, i: (row_blk0 + i, blk + h))
    st = pl.BlockSpec((None, group, None, LANES, LANES), lambda h, i: (0, i, h, 0, 0))
    return pl.pallas_call(
        functools.partial(_hgrn_sample_kernel, group=group),
        grid=(nh, nbs // group),
        in_specs=[col(blk_q), col(blk_f), col(blk_i), col(blk_g),
                  pl.BlockSpec((1, LANES), lambda h, i: (0, h)),
                  pl.BlockSpec((1, LANES), lambda h, i: (0, 0)),
                  st],
        out_specs=[pl.BlockSpec((group, LANES), lambda h, i: (i, h)), st],
        out_shape=[jax.ShapeDtypeStruct((nbs, nh * LANES), F32),
                   jax.ShapeDtypeStruct((1,) + state.shape[1:], F32)],
        compiler_params=_cparams(("parallel", "parallel")),
        name="hgrn_sample",
    )(proj, proj, proj, proj, lb, gain, state)


def _merge_kernel(of_ref, oh_ref, wf_ref, wh_ref, ga_ref, gb_ref, o_ref):
    a = _dot(of_ref[...], wf_ref[...])
    b = _dot(oh_ref[...], wh_ref[...])
    o_ref[...] = (_sigmoid(ga_ref[...]) * a + _sigmoid(gb_ref[...]) * b).astype(o_ref.dtype)


def _merge(o_fox, o_hg, wf, wh, proj, blk_ga, blk_gb, tm, tn):
    m, kf = o_fox.shape
    n = wf.shape[1]
    return pl.pallas_call(
        _merge_kernel,
        grid=(m // tm, n // tn),
        in_specs=[pl.BlockSpec((tm, kf), lambda i, j: (i, 0)),
                  pl.BlockSpec((tm, kf), lambda i, j: (i, 0)),
                  pl.BlockSpec((kf, tn), lambda i, j: (0, j)),
                  pl.BlockSpec((kf, tn), lambda i, j: (0, j)),
                  pl.BlockSpec((tm, tn), lambda i, j: (i, blk_ga + j)),
                  pl.BlockSpec((tm, tn), lambda i, j: (i, blk_gb + j))],
        out_specs=pl.BlockSpec((tm, tn), lambda i, j: (i, j)),
        out_shape=jax.ShapeDtypeStruct((m, n), BF16),
        compiler_params=_cparams(("parallel", "arbitrary")),
        name="branch_merge",
    )(o_fox, o_hg, wf, wh, proj, proj)


def _outproj_kernel(mg_ref, w_ref, x_ref, g_ref, wr_ref, br_ref, h_ref, hn_ref, lg_ref, hrow_ref, *, tn):
    j = pl.program_id(1)
    hblk = x_ref[...] + _dot(mg_ref[...], w_ref[...])
    h_ref[...] = hblk
    c0 = pl.multiple_of(j * tn, tn)
    hrow_ref[:, pl.ds(c0, tn)] = hblk

    @pl.when(j == pl.num_programs(1) - 1)
    def _():
        h = hrow_ref[...]
        ms = jnp.mean(h * h, axis=-1, keepdims=True)
        hn = (h * lax.rsqrt(ms + RMS_EPS) * g_ref[...]).astype(BF16)
        lg_ref[...] = _dot(hn, wr_ref[...]) + br_ref[...]
        half = hn.shape[1] // 2
        lo = lax.bitcast_convert_type(hn[:, :half].astype(F32), jnp.uint32) >> 16
        hi = lax.bitcast_convert_type(hn[:, half:].astype(F32), jnp.uint32)
        word = hi | lo
        nseg = half // LANES
        for s in range(nseg):
            hn_ref[pl.ds(s, hn.shape[0], stride=nseg), :] = word[:, s * LANES:(s + 1) * LANES]


def _outproj(merged, w_out, x, gain, w_router, b_router, tm, tn):
    m, d = x.shape
    ne = w_router.shape[1]
    return pl.pallas_call(
        functools.partial(_outproj_kernel, tn=tn),
        grid=(m // tm, d // tn),
        in_specs=[pl.BlockSpec((tm, d), lambda i, j: (i, 0)),
                  pl.BlockSpec((d, tn), lambda i, j: (0, j)),
                  pl.BlockSpec((tm, tn), lambda i, j: (i, j)),
                  pl.BlockSpec((1, d), lambda i, j: (0, 0)),
                  pl.BlockSpec((d, ne), lambda i, j: (0, 0)),
                  pl.BlockSpec((1, ne), lambda i, j: (0, 0))],
        out_specs=[pl.BlockSpec((tm, tn), lambda i, j: (i, j)),
                   pl.BlockSpec((tm * (d // 2 // LANES), LANES), lambda i, j: (i, 0)),
                   pl.BlockSpec((tm, ne), lambda i, j: (i, 0))],
        out_shape=[jax.ShapeDtypeStruct((m, d), F32),
                   jax.ShapeDtypeStruct((m * (d // 2 // LANES), LANES), jnp.uint32),
                   jax.ShapeDtypeStruct((m, ne), F32)],
        scratch_shapes=[pltpu.VMEM((tm, d), F32)],
        compiler_params=_cparams(("parallel", "arbitrary")),
        name="out_proj",
    )(merged, w_out, x, gain, w_router, b_router)


def _gather_kernel(tok_ref, va_ref, x_hbm, o_ref, buf, sem, *, rows, nseg):
    i = pl.program_id(0)
    last = pl.num_programs(0) - 1

    def issue(blk, slot):
        def start(r, carry):
            t = tok_ref[blk * rows + r]
            pltpu.make_async_copy(x_hbm.at[pl.ds(pl.multiple_of(t * nseg, nseg), nseg), :],
                                  buf.at[pl.ds(pl.multiple_of((slot * rows + r) * nseg, nseg), nseg), :],
                                  sem.at[slot]).start()
            return carry
        lax.fori_loop(0, rows, start, 0, unroll=8)

    @pl.when(jnp.logical_and(i == 0, va_ref[0] == 1))
    def _():
        issue(0, 0)

    nxt = jnp.minimum(i + 1, last)

    @pl.when(jnp.logical_and(i < last, va_ref[nxt] == 1))
    def _():
        issue(nxt, nxt % 2)

    slot = i % 2

    @pl.when(va_ref[i] == 1)
    def _():
        base = pl.multiple_of(slot * rows * nseg, rows * nseg)
        pltpu.make_async_copy(x_hbm.at[pl.ds(0, rows * nseg), :], buf.at[pl.ds(base, rows * nseg), :],
                              sem.at[slot]).wait()
        half = nseg * LANES
        for s in range(nseg):
            u = buf[pl.ds(base + s, rows, stride=nseg), :]
            o_ref[:, s * LANES:(s + 1) * LANES] = lax.bitcast_convert_type(u << 16, F32).astype(o_ref.dtype)
            o_ref[:, half + s * LANES:half + (s + 1) * LANES] = (
                lax.bitcast_convert_type(u & jnp.uint32(0xFFFF0000), F32).astype(o_ref.dtype))

    @pl.when(va_ref[i] == 0)
    def _():
        o_ref[...] = jnp.zeros(o_ref.shape, o_ref.dtype)


def _gather_rows(row_token, valid, x, rows, nseg):
    n_rows = row_token.shape[0]
    d = 2 * nseg * LANES
    assert x.shape[0] >= rows * nseg and x.shape[1] == LANES
    return pl.pallas_call(
        functools.partial(_gather_kernel, rows=rows, nseg=nseg),
        grid_spec=pltpu.PrefetchScalarGridSpec(
            num_scalar_prefetch=2,
            grid=(n_rows // rows,),
            in_specs=[pl.BlockSpec(memory_space=pl.ANY)],
            out_specs=pl.BlockSpec((rows, d), lambda i, tok, va: (i, 0)),
            scratch_shapes=[pltpu.VMEM((2 * rows * nseg, LANES), x.dtype), pltpu.SemaphoreType.DMA((2,))]),
        out_shape=jax.ShapeDtypeStruct((n_rows, d), BF16),
        compiler_params=_cparams(("arbitrary",)),
        name="moe_gather",
    )(row_token, valid, x)


WEIGHT_COPY_SPLIT = 8


class _WeightCopy:
    def __init__(self, w_hbm, e, col, width, stage, sem):
        self.w_hbm, self.e, self.col, self.width, self.stage, self.sem = w_hbm, e, col, width, stage, sem

    def start(self):
        rows = self.stage.shape[0] // WEIGHT_COPY_SPLIT
        cols = pl.ds(pl.multiple_of(self.col, LANES), self.width)
        for c in range(WEIGHT_COPY_SPLIT):
            pltpu.make_async_copy(self.w_hbm.at[self.e, pl.ds(c * rows, rows), cols],
                                  self.stage.at[pl.ds(c * rows, rows), :], self.sem).start()

    def wait(self):
        pltpu.make_async_copy(self.w_hbm.at[0, :, pl.ds(0, self.width)], self.stage, self.sem).wait()


def _weight_copy(w_hbm, e, col, width, stage, sem):
    return _WeightCopy(w_hbm, e, col, width, stage, sem)


def _stream_weights(ne_ref, nx_ref, lr_ref, be_ref, ri_ref, copies):
    j = pl.program_id(0)
    i = pl.program_id(1)
    slot = (j * ri_ref[ri_ref.shape[0] - 1] + ri_ref[i]) % 2

    @pl.when(ne_ref[i] == 1)
    def _():
        cur = copies(be_ref[i], j, slot)

        @pl.when(jnp.logical_and(j == 0, i == 0))
        def _():
            for c in cur:
                c.start()

        for c in cur:
            c.wait()
        jn = jnp.where(lr_ref[i] == 1, j + 1, j)

        @pl.when(jn < pl.num_programs(0))
        def _():
            for c in copies(nx_ref[i], jn, 1 - slot):
                c.start()

    return slot


def _moe_up_kernel(be_ref, ne_ref, va_ref, nx_ref, lr_ref, ri_ref, x_ref, w_hbm, bg_ref, bu_ref, o_ref,
                   sg, su, sem, *, tf, f):
    i = pl.program_id(1)

    def copies(e, jj, slot):
        return (_weight_copy(w_hbm, e, jj * tf, tf, sg.at[slot], sem.at[0, slot]),
                _weight_copy(w_hbm, e, f + jj * tf, tf, su.at[slot], sem.at[1, slot]))

    slot = _stream_weights(ne_ref, nx_ref, lr_ref, be_ref, ri_ref, copies)

    @pl.when(va_ref[i] == 1)
    def _():
        x = x_ref[...]
        g = jnp.minimum(_dot(x, sg[slot].astype(BF16)) + bg_ref[...], SWIGLU_LIMIT)
        u = jnp.clip(_dot(x, su[slot].astype(BF16)) + bu_ref[...], -SWIGLU_LIMIT, SWIGLU_LIMIT)
        o_ref[...] = ((u + 1.0) * g * _sigmoid(SWIGLU_ALPHA * g)).astype(o_ref.dtype)

    @pl.when(va_ref[i] == 0)
    def _():
        o_ref[...] = jnp.zeros(o_ref.shape, o_ref.dtype)


def _moe_up(sched, xs, w_gate_up, b_gate_up, rows, tf):
    n_rows, d = xs.shape
    f = w_gate_up.shape[2] // 2
    nf = f // tf
    return pl.pallas_call(
        functools.partial(_moe_up_kernel, tf=tf, f=f),
        grid_spec=pltpu.PrefetchScalarGridSpec(
            num_scalar_prefetch=6,
            grid=(nf, n_rows // rows),
            in_specs=[pl.BlockSpec((rows, d), lambda j, i, be, *_: (i, 0)),
                      pl.BlockSpec(memory_space=pl.ANY),
                      pl.BlockSpec((None, 1, tf), lambda j, i, be, *_: (be[i], 0, j)),
                      pl.BlockSpec((None, 1, tf), lambda j, i, be, *_: (be[i], 0, nf + j))],
            out_specs=pl.BlockSpec((rows, tf), lambda j, i, be, *_: (i, j)),
            scratch_shapes=[pltpu.VMEM((2, d, tf), F32), pltpu.VMEM((2, d, tf), F32),
                            pltpu.SemaphoreType.DMA((2, 2))]),
        out_shape=jax.ShapeDtypeStruct((n_rows, f), BF16),
        compiler_params=_cparams(("arbitrary", "arbitrary")),
        name="moe_up",
    )(*sched, xs, w_gate_up, b_gate_up, b_gate_up)


def _moe_down_kernel(be_ref, ne_ref, va_ref, nx_ref, lr_ref, ri_ref, a_ref, w_hbm, b_ref, o_ref, sw, sem, *, tn):
    i = pl.program_id(1)

    def copies(e, jj, slot):
        return (_weight_copy(w_hbm, e, jj * tn, tn, sw.at[slot], sem.at[slot]),)

    slot = _stream_weights(ne_ref, nx_ref, lr_ref, be_ref, ri_ref, copies)

    @pl.when(va_ref[i] == 1)
    def _():
        o_ref[...] = _dot(a_ref[...], sw[slot].astype(BF16)) + b_ref[...]

    @pl.when(va_ref[i] == 0)
    def _():
        o_ref[...] = jnp.zeros(o_ref.shape, o_ref.dtype)


def _moe_down(sched, act, w_down, b_down, rows, tn):
    n_rows, f = act.shape
    d = w_down.shape[2]
    return pl.pallas_call(
        functools.partial(_moe_down_kernel, tn=tn),
        grid_spec=pltpu.PrefetchScalarGridSpec(
            num_scalar_prefetch=6,
            grid=(d // tn, n_rows // rows),
            in_specs=[pl.BlockSpec((rows, f), lambda j, i, be, *_: (i, 0)),
                      pl.BlockSpec(memory_space=pl.ANY),
                      pl.BlockSpec((None, 1, tn), lambda j, i, be, *_: (be[i], 0, j))],
            out_specs=pl.BlockSpec((rows, tn), lambda j, i, be, *_: (i, j)),
            scratch_shapes=[pltpu.VMEM((2, f, tn), F32), pltpu.SemaphoreType.DMA((2,))]),
        out_shape=jax.ShapeDtypeStruct((n_rows, d), F32),
        compiler_params=_cparams(("arbitrary", "arbitrary")),
        name="moe_down",
    )(*sched, act, w_down, b_down)


def _combine_kernel(pos_ref, h_ref, gt_ref, g_ref, y_hbm, h2_ref, hn_ref, buf, sem, *, tm, top_k):
    i = pl.program_id(0)
    last = pl.num_programs(0) - 1
    n = tm * top_k

    def issue(blk, slot):
        def start(a, carry):
            src = pos_ref[blk * n + a]
            pltpu.make_async_copy(y_hbm.at[pl.ds(src, 1), :],
                                  buf.at[slot, a % top_k, pl.ds(a // top_k, 1), :], sem.at[slot]).start()
            return carry
        lax.fori_loop(0, n, start, 0, unroll=8)

    @pl.when(i == 0)
    def _():
        issue(0, 0)

    nxt = jnp.minimum(i + 1, last)

    @pl.when(i < last)
    def _():
        issue(nxt, nxt % 2)

    slot = i % 2
    for k in range(top_k):
        pltpu.make_async_copy(y_hbm.at[pl.ds(0, tm), :], buf.at[slot, k], sem.at[slot]).wait()
    h = h_ref[...]
    gt = gt_ref[...]
    for k in range(top_k):
        h = h + gt[:, k:k + 1] * buf[slot, k]
    h2_ref[...] = h
    ms = jnp.mean(h * h, axis=-1, keepdims=True)
    hn_ref[...] = (h * lax.rsqrt(ms + RMS_EPS) * g_ref[...]).astype(hn_ref.dtype)


def _combine(pos, h, gates, gain, y_rows, tm):
    m, d = h.shape
    assert y_rows.shape[0] >= tm
    return pl.pallas_call(
        functools.partial(_combine_kernel, tm=tm, top_k=TOP_K),
        grid_spec=pltpu.PrefetchScalarGridSpec(
            num_scalar_prefetch=1,
            grid=(m // tm,),
            in_specs=[pl.BlockSpec((tm, d), lambda i, pos: (i, 0)),
                      pl.BlockSpec((tm, TOP_K), lambda i, pos: (i, 0)),
                      pl.BlockSpec((1, d), lambda i, pos: (0, 0)),
                      pl.BlockSpec(memory_space=pl.ANY)],
            out_specs=[pl.BlockSpec((tm, d), lambda i, pos: (i, 0)),
                       pl.BlockSpec((tm, d), lambda i, pos: (i, 0))],
            scratch_shapes=[pltpu.VMEM((2, TOP_K, tm, d), F32), pltpu.SemaphoreType.DMA((2,))]),
        out_shape=[jax.ShapeDtypeStruct((m, d), F32), jax.ShapeDtypeStruct((m, d), BF16)],
        compiler_params=_cparams(("arbitrary",)),
        name="moe_combine",
    )(pos, h, gates, gain, y_rows)


def _ple_kernel(hn_ref, wg_ref, p_ref, wp_ref, h_ref, o_ref):
    gate = _sigmoid(_dot(hn_ref[...], wg_ref[...]))
    o_ref[...] = h_ref[...] + gate * _dot(p_ref[...], wp_ref[...])


def _ple(hn, w_gate, p, w_proj, h, row0, n, tm, tn):
    d = h.shape[1]
    pd = p.shape[1]
    r0 = row0 // tm
    assert row0 % tm == 0 and n % tm == 0
    return pl.pallas_call(
        _ple_kernel,
        grid=(n // tm, d // tn),
        in_specs=[pl.BlockSpec((tm, d), lambda i, j: (r0 + i, 0)),
                  pl.BlockSpec((d, tn), lambda i, j: (0, j)),
                  pl.BlockSpec((tm, pd), lambda i, j: (i, 0)),
                  pl.BlockSpec((pd, tn), lambda i, j: (0, j)),
                  pl.BlockSpec((tm, tn), lambda i, j: (r0 + i, j))],
        out_specs=pl.BlockSpec((tm, tn), lambda i, j: (i, j)),
        out_shape=jax.ShapeDtypeStruct((n, d), F32),
        compiler_params=_cparams(("parallel", "arbitrary")),
        name="ple",
    )(hn, w_gate, p, w_proj, h)


def _route(logits, n_experts, rows):
    n = logits.shape[0]
    n_assign = n * TOP_K
    n_blocks = (n_assign + n_experts * (rows - 1) + rows - 1) // rows
    n_rows = n_blocks * rows
    top_logits, top_idx = lax.top_k(logits, TOP_K)
    gates = jax.nn.softmax(top_logits, axis=-1)
    expert = top_idx.reshape(-1).astype(jnp.int32)
    token = jnp.arange(n_assign, dtype=jnp.int32) // TOP_K
    onehot = expert[:, None] == jnp.arange(n_experts, dtype=jnp.int32)[None, :]
    cb = _divisor(n_assign, 256, 8)
    oh = onehot.reshape(n_assign // cb, cb, n_experts).astype(BF16)
    tri = (jnp.arange(cb)[:, None] >= jnp.arange(cb)[None, :]).astype(BF16)
    within = jnp.einsum('ij,bjk->bik', tri, oh, preferred_element_type=F32)
    totals = within[:, -1, :]
    offsets = jnp.cumsum(totals, axis=0) - totals
    csum = (within + offsets[:, None, :]).reshape(n_assign, n_experts)
    rank = jnp.sum(jnp.where(onehot, csum, 0.0), axis=1).astype(jnp.int32) - 1
    counts = (offsets[-1] + totals[-1]).astype(jnp.int32)
    padded = (counts + rows - 1) // rows * rows
    pend = jnp.cumsum(padded)
    pos = (pend - padded)[expert] + rank
    row_token = jnp.zeros((n_rows,), jnp.int32).at[pos].set(token)
    blk_start = jnp.arange(n_blocks, dtype=jnp.int32) * rows
    block_expert = jnp.minimum(jnp.sum((blk_start[:, None] >= pend[None, :]).astype(jnp.int32), axis=1), n_experts - 1)
    valid = (blk_start < pend[-1]).astype(jnp.int32)
    last_valid = block_expert[jnp.maximum(pend[-1] // rows - 1, 0)]
    block_expert = jnp.where(valid == 1, block_expert, last_valid)
    prev = jnp.concatenate([jnp.full((1,), -1, jnp.int32), block_expert[:-1]])
    new_expert = (block_expert != prev).astype(jnp.int32)
    run_index = jnp.cumsum(new_expert) - 1
    run_info = jnp.concatenate([run_index, run_index[-1:] + 1]).astype(jnp.int32)
    after = jnp.sum((block_expert[None, :] <= block_expert[:, None]).astype(jnp.int32), axis=1)
    last_run = (after >= n_blocks).astype(jnp.int32)
    next_expert = block_expert[jnp.where(after >= n_blocks, 0, after)]
    sched = (block_expert, new_expert, valid, next_expert, last_run, run_info)
    return pos.astype(jnp.int32), gates, row_token, sched


def kernel(x_prompt, x_sample, cache_k, cache_v, cache_logf, state_hgrn, page_table, p_prompt, p_sample,
           attn_norm, w_in, fox_fgate_bias, q_norm, k_norm, hg_lower_bound, hg_out_norm, w_proj_fox,
           w_proj_hg, w_out, ffn_norm, w_router, b_router, w_gate_up, b_gate_up, w_down, b_down,
           ple_norm, w_ple_gate, w_ple_proj):
    nb, t, d = x_prompt.shape
    nbs = x_sample.shape[0]
    assert x_sample.shape[1] == 1 and cache_k.shape[0] == 1
    _, _, page, nh, dh = cache_k.shape
    assert dh == LANES
    fw = nh * dh
    nhg = state_hgrn.shape[2]
    hw = nhg * LANES
    assert state_hgrn.shape[3] == LANES and state_hgrn.shape[4] == LANES and hw == fw
    n_experts = w_router.shape[2]
    mp = nb * t
    m = mp + nbs

    tm = _divisor(m, 704, 16)
    tn = _divisor(fw, 512, LANES)
    tq = _divisor(t, 512, LANES)
    chunk = LANES
    group = 8
    ts = _divisor(nbs, 128, 16)
    tp = _divisor(mp, 512, 16)
    hp = 4 if nhg % 4 == 0 and (3 * fw // LANES) % 4 == 0 else (2 if nhg % 2 == 0 else 1)
    pps = _divisor(page_table.shape[1], 4, 1)
    moe_rows = 256 if m * TOP_K // n_experts >= 512 else 16
    tm_out = tm
    tf = _divisor(d, 512, LANES)
    tdn = _divisor(d, 1024, LANES)
    assert nbs % group == 0 and mp % group == 0 and t % chunk == 0

    x = jnp.concatenate([x_prompt.reshape(mp, d), x_sample.reshape(nbs, d)], axis=0)
    w = w_in[0]
    ff0 = 3 * fw
    w_al = jnp.concatenate([w[:, :ff0], w[:, ff0 + nh:],
                            jnp.pad(w[:, ff0:ff0 + nh], ((0, 0), (0, tn - nh)))], axis=1).astype(BF16)
    blk = lambda col: col // LANES
    c_hq, c_hf, c_hi, c_hg = ff0, ff0 + hw, ff0 + 2 * hw, ff0 + 3 * hw
    c_ga = ff0 + 4 * hw
    c_gb = c_ga + d
    c_ff = c_gb + d
    lb = jax.nn.softmax(hg_lower_bound.astype(F32), axis=0)[0:1]
    fbias = jnp.pad(fox_fgate_bias, ((0, 0), (0, LANES - nh)))

    proj = _inproj(x, attn_norm, w_al, tm, tn)
    qn_p, kn_p, vn_p, logf_p = _prep(proj, q_norm, k_norm, fbias, 0, mp, tp, fw, blk(c_ff), LANES)
    qn_s, kn_s, vn_s, logf_s = _prep(proj, q_norm, k_norm, fbias, mp, nbs, ts, fw, blk(c_ff), LANES)

    c = _seq_cumsum(logf_p, nb, t)
    c_t = c[:, :nh].reshape(nb, t, nh).transpose(0, 2, 1).reshape(nb * nh, 1, t)
    o_fox_p = _fox_attn(qn_p, kn_p, vn_p, c, c_t, nb, t, nh, tq)

    lf_new = logf_s[:, :nh].reshape(nbs, 1, nh)
    bias = _page_bias(page_table, jnp.swapaxes(cache_logf, 2, 3), lf_new)
    bias = bias.reshape(nbs, page_table.shape[1], page * nh // LANES, LANES)
    o_fox_s = _paged_attn(page_table, qn_s.reshape(nbs, nh, dh), kn_s.reshape(nbs, nh, dh),
                          vn_s.reshape(nbs, nh, dh), cache_k, cache_v, bias, pps)

    o_hg_p, st_p = _hgrn_prompt(proj, lb, hg_out_norm, nb, t, nhg, blk(c_hq), blk(c_hf), blk(c_hi), blk(c_hg),
                                chunk, hp)
    o_hg_s, st_s = _hgrn_sample(proj, state_hgrn, lb, hg_out_norm, mp // group, nbs, nhg,
                                blk(c_hq), blk(c_hf), blk(c_hi), blk(c_hg), group)

    o_fox = jnp.concatenate([o_fox_p, o_fox_s.reshape(nbs, fw).astype(BF16)], axis=0)
    o_hg = jnp.concatenate([o_hg_p, o_hg_s.astype(BF16)], axis=0)
    merged = _merge(o_fox, o_hg, w_proj_fox[0].astype(BF16), w_proj_hg[0].astype(BF16), proj,
                    c_ga // tn, c_gb // tn, tm, tn)

    ne_pad = -(-n_experts // LANES) * LANES
    w_r = jnp.pad(w_router[0], ((0, 0), (0, ne_pad - n_experts))).astype(BF16)
    b_r = jnp.pad(b_router, ((0, 0), (0, ne_pad - n_experts)))
    h, hn, logits = _outproj(merged, w_out[0].astype(BF16), x, ffn_norm, w_r, b_r, tm_out, tn)

    pos, gates, row_token, sched = _route(logits[:, :n_experts], n_experts, moe_rows)
    xs = _gather_rows(row_token, sched[2], hn, moe_rows, d // 2 // LANES)
    act = _moe_up(sched, xs, w_gate_up.reshape(w_gate_up.shape[1:]), b_gate_up.reshape(n_experts, 1, -1),
                  moe_rows, tf)
    y_rows = _moe_down(sched, act, w_down.reshape(w_down.shape[1:]), b_down.reshape(n_experts, 1, -1),
                       moe_rows, tdn)
    h2, hn2 = _combine(pos, h, gates, ple_norm, y_rows, _divisor(m, 128, 16))

    w_pg = w_ple_gate[0].astype(BF16)
    w_pp = w_ple_proj[0].astype(BF16)
    y_p = _ple(hn2, w_pg, p_prompt[0].reshape(mp, -1).astype(BF16), w_pp, h2, 0, mp, tp, tn)
    y_s = _ple(hn2, w_pg, p_sample[0].reshape(nbs, -1).astype(BF16), w_pp, h2, mp, nbs, ts, tn)

    return (y_p.reshape(nb, t, d), y_s.reshape(nbs, 1, d),
            kn_p.reshape(1, nb, t, nh, dh), vn_p.reshape(1, nb, t, nh, dh), logf_p[:, :nh].reshape(1, nb, t, nh),
            st_p[None],
            kn_s.reshape(1, nbs, 1, nh, dh), vn_s.reshape(1, nbs, 1, nh, dh), logf_s[:, :nh].reshape(1, nbs, 1, nh),
            st_s)
```

```python
import functools

import jax
import jax.numpy as jnp
from jax import lax
from jax.experimental import pallas as pl
from jax.experimental.pallas import tpu as pltpu

F32 = jnp.float32
BF16 = jnp.bfloat16

TOP_K = 4
SWIGLU_LIMIT = 7.0
SWIGLU_ALPHA = 1.702
RMS_EPS = 1e-6
LANES = 128
MASK_VALUE = -1e30
VMEM_LIMIT = 56 * 1024 * 1024


def _cparams(sem):
    return pltpu.CompilerParams(dimension_semantics=sem, vmem_limit_bytes=VMEM_LIMIT)


def _divisor(n, cap, mult):
    best = None
    for d in range(mult, min(n, cap) + 1, mult):
        if n % d == 0:
            best = d
    assert best is not None, (n, cap, mult)
    return best


def _dot(a, b):
    return jnp.dot(a, b, preferred_element_type=F32)


def _dot_nt(a, b):
    return lax.dot_general(a, b, (((1,), (1,)), ((), ())), preferred_element_type=F32)


def _dot_tn(a, b):
    return lax.dot_general(a, b, (((0,), (0,)), ((), ())), preferred_element_type=F32)


def _split3(x):
    x1 = x.astype(BF16)
    r1 = x - x1.astype(F32)
    x2 = r1.astype(BF16)
    x3 = (r1 - x2.astype(F32)).astype(BF16)
    return x1, x2, x3


def _sel_dot(sel, x):
    x1, x2, x3 = _split3(x)
    return _dot(sel, x1) + _dot(sel, x2) + _dot(sel, x3)


def _sigmoid(x):
    return 1.0 / (1.0 + jnp.exp(-x))


def _iota(shape, dim):
    return lax.broadcasted_iota(jnp.int32, shape, dim)


def _inproj_kernel(x_ref, g_ref, w_ref, o_ref, xn_ref):
    @pl.when(pl.program_id(1) == 0)
    def _():
        x = x_ref[...]
        ms = jnp.mean(x * x, axis=-1, keepdims=True)
        xn_ref[...] = (x * lax.rsqrt(ms + RMS_EPS) * g_ref[...]).astype(BF16)

    o_ref[...] = _dot(xn_ref[...], w_ref[...])


def _inproj(x, gain, w, tm, tn):
    m, d = x.shape
    n = w.shape[1]
    return pl.pallas_call(
        _inproj_kernel,
        grid=(m // tm, n // tn),
        in_specs=[pl.BlockSpec((tm, d), lambda i, j: (i, 0)),
                  pl.BlockSpec((1, d), lambda i, j: (0, 0)),
                  pl.BlockSpec((d, tn), lambda i, j: (0, j))],
        out_specs=pl.BlockSpec((tm, tn), lambda i, j: (i, j)),
        out_shape=jax.ShapeDtypeStruct((m, n), F32),
        scratch_shapes=[pltpu.VMEM((tm, d), BF16)],
        compiler_params=_cparams(("parallel", "arbitrary")),
        name="inproj",
    )(x, gain, w)


def _prep_kernel(fq_ref, fk_ref, fv_ref, ff_ref, qg_ref, kg_ref, fb_ref, q_ref, k_ref, v_ref, lf_ref, *, nh):
    qg = qg_ref[...]
    kg = kg_ref[...]
    for h in range(nh):
        sl = slice(h * LANES, (h + 1) * LANES)
        for src, gain, dst in ((fq_ref, qg, q_ref), (fk_ref, kg, k_ref)):
            xh = src[:, sl]
            ms = jnp.mean(xh * xh, axis=-1, keepdims=True)
            dst[:, sl] = xh * lax.rsqrt(ms + RMS_EPS) * gain
    v_ref[...] = fv_ref[...]
    z = ff_ref[...] + fb_ref[...]
    lf_ref[...] = jnp.minimum(z, 0.0) - jnp.log1p(jnp.exp(-jnp.abs(z)))


def _prep(proj, q_gain, k_gain, fbias, row0, n, tm, fw, ff_blk, ffw):
    nh = fw // LANES
    r0 = row0 // tm
    assert row0 % tm == 0 and n % tm == 0
    return pl.pallas_call(
        functools.partial(_prep_kernel, nh=nh),
        grid=(n // tm,),
        in_specs=[pl.BlockSpec((tm, fw), lambda i: (r0 + i, 0)),
                  pl.BlockSpec((tm, fw), lambda i: (r0 + i, 1)),
                  pl.BlockSpec((tm, fw), lambda i: (r0 + i, 2)),
                  pl.BlockSpec((tm, ffw), lambda i: (r0 + i, ff_blk)),
                  pl.BlockSpec((1, LANES), lambda i: (0, 0)),
                  pl.BlockSpec((1, LANES), lambda i: (0, 0)),
                  pl.BlockSpec((1, ffw), lambda i: (0, 0))],
        out_specs=[pl.BlockSpec((tm, fw), lambda i: (i, 0)),
                   pl.BlockSpec((tm, fw), lambda i: (i, 0)),
                   pl.BlockSpec((tm, fw), lambda i: (i, 0)),
                   pl.BlockSpec((tm, ffw), lambda i: (i, 0))],
        out_shape=[jax.ShapeDtypeStruct((n, fw), F32),
                   jax.ShapeDtypeStruct((n, fw), F32),
                   jax.ShapeDtypeStruct((n, fw), F32),
                   jax.ShapeDtypeStruct((n, ffw), F32)],
        compiler_params=_cparams(("parallel",)),
        name="fox_prep",
    )(proj, proj, proj, proj, q_gain, k_gain, fbias)


def _cumsum_kernel(x_ref, o_ref, *, t, c):
    tri = (_iota((c, c), 0) >= _iota((c, c), 1)).astype(BF16)
    carry = jnp.zeros((1, x_ref.shape[1]), F32)
    for n in range(t // c):
        blk = _sel_dot(tri, x_ref[n * c:(n + 1) * c, :]) + carry
        o_ref[n * c:(n + 1) * c, :] = blk
        carry = blk[c - 1:c, :]


def _seq_cumsum(x, nb, t):
    w = x.shape[1]
    return pl.pallas_call(
        functools.partial(_cumsum_kernel, t=t, c=LANES),
        grid=(nb,),
        in_specs=[pl.BlockSpec((t, w), lambda b: (b, 0))],
        out_specs=pl.BlockSpec((t, w), lambda b: (b, 0)),
        out_shape=jax.ShapeDtypeStruct((nb * t, w), F32),
        compiler_params=_cparams(("parallel",)),
        name="logf_cumsum",
    )(x)


def _fox_attn_kernel(q_ref, k_ref, v_ref, cc_ref, cr_ref, o_ref, *, tq, scale, hpa):
    hg = pl.program_id(1)
    i = pl.program_id(2)
    lane = _iota(cc_ref.shape, 1)
    cc = cc_ref[...]
    causal = _iota((tq, tq), 0) >= _iota((tq, tq), 1)
    qs, cqs = [], []
    for hd in range(hpa):
        qs.append(q_ref[:, hd * LANES:(hd + 1) * LANES].astype(BF16))
        cqs.append(jnp.sum(jnp.where(lane == hg * hpa + hd, cc, 0.0), axis=-1, keepdims=True))

    def head_block(hd, r0, carry, diagonal):
        m, l, acc = carry
        sl = slice(hd * LANES, (hd + 1) * LANES)
        ks = k_ref[pl.ds(r0, tq), sl].astype(BF16)
        vs = v_ref[pl.ds(r0, tq), sl].astype(BF16)
        ck = cr_ref[hd, :, pl.ds(r0, tq)]
        s = _dot_nt(qs[hd], ks) * scale + (cqs[hd] - ck)
        if diagonal:
            s = jnp.where(causal, s, MASK_VALUE)
        m_new = jnp.maximum(m, jnp.max(s, axis=-1, keepdims=True))
        alpha = jnp.exp(m - m_new)
        p = jnp.exp(s - m_new)
        l = alpha * l + jnp.sum(p, axis=-1, keepdims=True)
        acc = alpha * acc + _dot(p.astype(BF16), vs)
        return m_new, l, acc

    def block(j, carries, diagonal):
        r0 = pl.multiple_of(j * tq, tq)
        return tuple(head_block(hd, r0, carries[hd], diagonal) for hd in range(hpa))

    init = tuple((jnp.full((tq, 1), MASK_VALUE, F32), jnp.zeros((tq, 1), F32), jnp.zeros((tq, LANES), F32))
                 for _ in range(hpa))
    carries = lax.fori_loop(0, i, lambda j, c: block(j, c, False), init)
    carries = block(i, carries, True)
    for hd in range(hpa):
        _, l, acc = carries[hd]
        o_ref[:, hd * LANES:(hd + 1) * LANES] = (acc / l).astype(o_ref.dtype)


def _fox_attn(qn, kn, vn, c, c_t, nb, t, nh, tq, hpa):
    nq = t // tq
    cw = c.shape[1]
    w = hpa * LANES
    return pl.pallas_call(
        functools.partial(_fox_attn_kernel, tq=tq, scale=LANES ** -0.5, hpa=hpa),
        grid=(nb, nh // hpa, nq),
        in_specs=[pl.BlockSpec((tq, w), lambda b, h, i: (b * nq + i, h)),
                  pl.BlockSpec((t, w), lambda b, h, i: (b, h)),
                  pl.BlockSpec((t, w), lambda b, h, i: (b, h)),
                  pl.BlockSpec((tq, cw), lambda b, h, i: (b * nq + i, 0)),
                  pl.BlockSpec((hpa, 1, t), lambda b, h, i: (b * (nh // hpa) + h, 0, 0))],
        out_specs=pl.BlockSpec((tq, w), lambda b, h, i: (b * nq + i, h)),
        out_shape=jax.ShapeDtypeStruct((nb * t, nh * LANES), BF16),
        compiler_params=_cparams(("parallel", "parallel", "arbitrary")),
        name="fox_attn",
    )(qn, kn, vn, c, c_t)


def _page_bias_kernel(pt_ref, *refs, n_pages, page):
    lf_refs = refs[:n_pages]
    new_ref = refs[n_pages]
    o_ref = refs[n_pages + 1]
    sel = ((_iota((page + 8, page), 0) < _iota((page + 8, page), 1)) | (_iota((page + 8, page), 0) >= page)).astype(BF16)
    carry = new_ref[...]
    for p in range(n_pages - 1, -1, -1):
        x1, x2, x3 = _split3(lf_refs[p][...])
        sums = _dot_nt(sel, x1) + _dot_nt(sel, x2) + _dot_nt(sel, x3)
        o_ref[p * page:(p + 1) * page, :] = sums[:page, :] + carry
        carry = carry + sums[page:page + 1, :]


def _page_bias(page_table, cache_logf_t, lf_new):
    nbs, n_pages = page_table.shape
    _, _, nh, page = cache_logf_t.shape
    in_specs = [pl.BlockSpec((None, None, nh, page),
                             functools.partial(lambda b, pt, p: (0, pt[b * n_pages + p], 0, 0), p=p))
                for p in range(n_pages)]
    in_specs.append(pl.BlockSpec((None, 1, nh), lambda b, pt: (b, 0, 0)))
    return pl.pallas_call(
        functools.partial(_page_bias_kernel, n_pages=n_pages, page=page),
        grid_spec=pltpu.PrefetchScalarGridSpec(
            num_scalar_prefetch=1,
            grid=(nbs,),
            in_specs=in_specs,
            out_specs=pl.BlockSpec((None, n_pages * page, nh), lambda b, pt: (b, 0, 0))),
        out_shape=jax.ShapeDtypeStruct((nbs, n_pages * page, nh), F32),
        compiler_params=_cparams(("arbitrary",)),
        name="page_bias",
    )(page_table.reshape(-1), *([cache_logf_t] * n_pages), lf_new)


def _paged_attn_kernel(pt_ref, q_ref, kn_ref, vn_ref, *refs, scale, page, nh, pps):
    k_refs, v_refs = refs[:pps], refs[pps:2 * pps]
    b_ref, o_ref, m_ref, l_ref, acc_ref = refs[2 * pps:]
    p = pl.program_id(1)
    q = q_ref[...]

    @pl.when(p == 0)
    def _():
        s_new = jnp.sum(q * kn_ref[...], axis=-1, keepdims=True) * scale
        m_ref[...] = jnp.broadcast_to(s_new, m_ref.shape)
        l_ref[...] = jnp.ones(l_ref.shape, F32)
        acc_ref[...] = vn_ref[...]

    rows = page * nh
    groups = rows // LANES
    eye = _iota((LANES, LANES), 0) == _iota((LANES, LANES), 1)
    ones = jnp.ones((2 * LANES, LANES), BF16)
    qs = q * scale
    scores = []
    for k_ref, g in zip(k_refs, range(pps)):
        kq = (k_ref[...] * qs[None]).reshape(groups, LANES, LANES)
        z = (kq + jnp.where(eye[None], b_ref[g][:, None, :], 0.0)).reshape(rows, LANES)
        hi = z.astype(BF16)
        lo = (z - hi.astype(F32)).astype(BF16)
        scores.append(_dot(jnp.concatenate([hi, lo], axis=1), ones).reshape(page, nh, LANES))
    m_old = m_ref[...]
    m_new = m_old
    for s in scores:
        m_new = jnp.maximum(m_new, jnp.max(s, axis=0))
    alpha = jnp.exp(m_old - m_new)
    l = alpha * l_ref[...]
    acc = alpha * acc_ref[...]
    for s, v_ref in zip(scores, v_refs):
        pr = jnp.exp(s - m_new[None])
        l = l + jnp.sum(pr, axis=0)
        acc = acc + jnp.sum(pr * v_ref[...], axis=0)
    l_ref[...] = l
    acc_ref[...] = acc
    m_ref[...] = m_new

    @pl.when(p == pl.num_programs(1) - 1)
    def _():
        o_ref[...] = acc_ref[...] / l_ref[...]


def _paged_attn(page_table, q, k_new, v_new, cache_k, cache_v, bias, pps):
    nbs, n_pages = page_table.shape
    _, _, page, nh, dh = cache_k.shape
    groups = page * nh // LANES
    tok = pl.BlockSpec((None, nh, dh), lambda b, p, pt: (b, 0, 0))
    pgs = [pl.BlockSpec((None, None, page, nh, dh),
                        functools.partial(lambda b, p, pt, g: (0, pt[b * n_pages + p * pps + g], 0, 0, 0), g=g))
           for g in range(pps)]
    return pl.pallas_call(
        functools.partial(_paged_attn_kernel, scale=dh ** -0.5, page=page, nh=nh, pps=pps),
        grid_spec=pltpu.PrefetchScalarGridSpec(
            num_scalar_prefetch=1,
            grid=(nbs, n_pages // pps),
            in_specs=[tok, tok, tok] + pgs + pgs +
                     [pl.BlockSpec((None, pps, groups, LANES), lambda b, p, pt: (b, p, 0, 0))],
            out_specs=tok,
            scratch_shapes=[pltpu.VMEM((nh, dh), F32)] * 3),
        out_shape=jax.ShapeDtypeStruct((nbs, nh, dh), F32),
        compiler_params=_cparams(("parallel", "arbitrary")),
        name="paged_attn",
    )(page_table.reshape(-1), q, k_new, v_new, *([cache_k] * pps), *([cache_v] * pps), bias)


def _hgrn_levels(c):
    levels = []
    m = 1
    while m < c:
        levels.append(m)
        m *= 2
    return levels


def _hgrn_prompt_kernel(hq_ref, hf_ref, hi_ref, hg_ref, lb_ref, gain_ref, o_ref, st_ref, *, t, c, hp):
    levels = _hgrn_levels(c)
    row = _iota((c, c), 0)
    col = _iota((c, c), 1)
    eye = row == col
    cums, masks = [(row >= col).astype(BF16)], []
    for m in levels:
        boundary = (row // (2 * m)) * (2 * m) + (m - 1)
        if 2 * m < 16:
            cums.append((col <= boundary).astype(BF16))
        same = (row // (2 * m)) == (col // (2 * m))
        masks.append(same & ((row % (2 * m)) >= m) & ((col % (2 * m)) < m))
    n_mm = len(cums) - 1
    cum_all = jnp.concatenate(cums, axis=0)
    gain = gain_ref[...]

    def head_chunk(r0, hd, s_t):
        sl = slice(hd * LANES, (hd + 1) * LANES)
        lb = lb_ref[:, sl]
        hq = hq_ref[pl.ds(r0, c), sl]
        fg = lb + (1.0 - lb) * _sigmoid(hf_ref[pl.ds(r0, c), sl])
        q = hq * _sigmoid(hq)
        k = 1.0 - fg
        v = hi_ref[pl.ds(r0, c), sl].astype(BF16)
        sums = _sel_dot(cum_all, jnp.log(fg))
        b = sums[:c, :]
        a = jnp.where(eye, _dot_nt(q.astype(BF16), k.astype(BF16)), 0.0)
        for li, m in enumerate(levels):
            if li < n_mm:
                rl = sums[(li + 1) * c:(li + 2) * c, :]
            else:
                b3 = b.reshape(c // (2 * m), 2 * m, LANES)
                rl = jnp.broadcast_to(b3[:, m - 1:m, :], b3.shape).reshape(c, LANES)
            ql = q * jnp.exp(jnp.minimum(b - rl, 0.0))
            kl = k * jnp.exp(jnp.minimum(rl - b, 0.0))
            a = a + jnp.where(masks[li], _dot_nt(ql.astype(BF16), kl.astype(BF16)), 0.0)
        o = _dot(a.astype(BF16), v) + _dot_nt((q * jnp.exp(b)).astype(BF16), s_t.astype(BF16))
        b_last = b[c - 1:c, :]
        kd = k * jnp.exp(b_last - b)
        s_t = s_t * jnp.exp(b_last) + _dot_tn(v, kd.astype(BF16))
        ms = jnp.mean(o * o, axis=-1, keepdims=True)
        hg = hg_ref[pl.ds(r0, c), sl]
        o_ref[pl.ds(r0, c), sl] = (o * lax.rsqrt(ms + RMS_EPS) * gain * (hg * _sigmoid(hg))).astype(o_ref.dtype)
        return s_t

    def chunk(n, states):
        r0 = pl.multiple_of(n * c, c)
        return tuple(head_chunk(r0, hd, states[hd]) for hd in range(hp))

    states = lax.fori_loop(0, t // c, chunk, tuple(jnp.zeros((LANES, LANES), F32) for _ in range(hp)))
    ident = (_iota((LANES, LANES), 0) == _iota((LANES, LANES), 1)).astype(BF16)
    for hd in range(hp):
        s1, s2, s3 = _split3(states[hd])
        st_ref[hd] = _dot_nt(ident, s1) + _dot_nt(ident, s2) + _dot_nt(ident, s3)


def _hgrn_prompt(proj, lb, gain, nb, t, nh, blk_q, blk_f, blk_i, blk_g, c, hp):
    w = hp * LANES

    def col(blk):
        assert blk % hp == 0
        return pl.BlockSpec((t, w), lambda b, h: (b, blk // hp + h))
    return pl.pallas_call(
        functools.partial(_hgrn_prompt_kernel, t=t, c=c, hp=hp),
        grid=(nb, nh // hp),
        in_specs=[col(blk_q), col(blk_f), col(blk_i), col(blk_g),
                  pl.BlockSpec((1, w), lambda b, h: (0, h)),
                  pl.BlockSpec((1, LANES), lambda b, h: (0, 0))],
        out_specs=[pl.BlockSpec((t, w), lambda b, h: (b, h)),
                   pl.BlockSpec((None, hp, LANES, LANES), lambda b, h: (b, h, 0, 0))],
        out_shape=[jax.ShapeDtypeStruct((nb * t, nh * LANES), BF16),
                   jax.ShapeDtypeStruct((nb, nh, LANES, LANES), F32)],
        compiler_params=_cparams(("parallel", "parallel")),
        name="hgrn_prompt",
    )(proj, proj, proj, proj, lb, gain)


def _hgrn_sample_kernel(hq_ref, hf_ref, hi_ref, hg_ref, lb_ref, gain_ref, s_ref, o_ref, so_ref, *, group):
    eye = _iota((LANES, LANES), 0) == _iota((LANES, LANES), 1)

    def column(rowvec):
        return jnp.sum(jnp.where(eye, jnp.broadcast_to(rowvec, (LANES, LANES)), 0.0), axis=-1, keepdims=True)

    lb = lb_ref[...]
    gain = gain_ref[...]
    for s in range(group):
        hq = hq_ref[s:s + 1, :]
        fg = lb + (1.0 - lb) * _sigmoid(hf_ref[s:s + 1, :])
        q = hq * _sigmoid(hq)
        s_new = column(fg) * s_ref[s] + column(1.0 - fg) * hi_ref[s:s + 1, :]
        so_ref[s] = s_new
        o = jnp.sum(column(q) * s_new, axis=0, keepdims=True)
        ms = jnp.mean(o * o, axis=-1, keepdims=True)
        hg = hg_ref[s:s + 1, :]
        o_ref[s:s + 1, :] = o * lax.rsqrt(ms + RMS_EPS) * gain * (hg * _sigmoid(hg))


def _hgrn_sample(proj, state, lb, gain, row_blk0, nbs, nh, blk_q, blk_f, blk_i, blk_g, group):
    def col(blk):
        return pl.BlockSpec((group, LANES), lambda h, i: (row_blk0 + i, blk + h))
    st = pl.BlockSpec((None, group, None, LANES, LANES), lambda h, i: (0, i, h, 0, 0))
    return pl.pallas_call(
        functools.partial(_hgrn_sample_kernel, group=group),
        grid=(nh, nbs // group),
        in_specs=[col(blk_q), col(blk_f), col(blk_i), col(blk_g),
                  pl.BlockSpec((1, LANES), lambda h, i: (0, h)),
                  pl.BlockSpec((1, LANES), lambda h, i: (0, 0)),
                  st],
        out_specs=[pl.BlockSpec((group, LANES), lambda h, i: (i, h)), st],
        out_shape=[jax.ShapeDtypeStruct((nbs, nh * LANES), F32),
                   jax.ShapeDtypeStruct((1,) + state.shape[1:], F32)],
        compiler_params=_cparams(("parallel", "parallel")),
        name="hgrn_sample",
    )(proj, proj, proj, proj, lb, gain, state)


def _merge_kernel(of_ref, oh_ref, wf_ref, wh_ref, ga_ref, gb_ref, o_ref):
    a = _dot(of_ref[...], wf_ref[...])
    b = _dot(oh_ref[...], wh_ref[...])
    o_ref[...] = (_sigmoid(ga_ref[...]) * a + _sigmoid(gb_ref[...]) * b).astype(o_ref.dtype)


def _merge(o_fox, o_hg, wf, wh, proj, blk_ga, blk_gb, tm, tn):
    m, kf = o_fox.shape
    n = wf.shape[1]
    return pl.pallas_call(
        _merge_kernel,
        grid=(m // tm, n // tn),
        in_specs=[pl.BlockSpec((tm, kf), lambda i, j: (i, 0)),
                  pl.BlockSpec((tm, kf), lambda i, j: (i, 0)),
                  pl.BlockSpec((kf, tn), lambda i, j: (0, j)),
                  pl.BlockSpec((kf, tn), lambda i, j: (0, j)),
                  pl.BlockSpec((tm, tn), lambda i, j: (i, blk_ga + j)),
                  pl.BlockSpec((tm, tn), lambda i, j: (i, blk_gb + j))],
        out_specs=pl.BlockSpec((tm, tn), lambda i, j: (i, j)),
        out_shape=jax.ShapeDtypeStruct((m, n), BF16),
        compiler_params=_cparams(("parallel", "arbitrary")),
        name="branch_merge",
    )(o_fox, o_hg, wf, wh, proj, proj)


def _outproj_kernel(mg_ref, w_ref, x_ref, g_ref, wr_ref, br_ref, h_ref, hn_ref, lg_ref, hrow_ref, *, tn):
    j = pl.program_id(1)
    hblk = x_ref[...] + _dot(mg_ref[...], w_ref[...])
    h_ref[...] = hblk
    c0 = pl.multiple_of(j * tn, tn)
    hrow_ref[:, pl.ds(c0, tn)] = hblk

    @pl.when(j == pl.num_programs(1) - 1)
    def _():
        h = hrow_ref[...]
        ms = jnp.mean(h * h, axis=-1, keepdims=True)
        hn = (h * lax.rsqrt(ms + RMS_EPS) * g_ref[...]).astype(BF16)
        lg_ref[...] = _dot(hn, wr_ref[...]) + br_ref[...]
        half = hn.shape[1] // 2
        lo = lax.bitcast_convert_type(hn[:, :half].astype(F32), jnp.uint32) >> 16
        hi = lax.bitcast_convert_type(hn[:, half:].astype(F32), jnp.uint32)
        word = hi | lo
        nseg = half // LANES
        for s in range(nseg):
            hn_ref[pl.ds(s, hn.shape[0], stride=nseg), :] = word[:, s * LANES:(s + 1) * LANES]


def _outproj(merged, w_out, x, gain, w_router, b_router, tm, tn):
    m, d = x.shape
    ne = w_router.shape[1]
    return pl.pallas_call(
        functools.partial(_outproj_kernel, tn=tn),
        grid=(m // tm, d // tn),
        in_specs=[pl.BlockSpec((tm, d), lambda i, j: (i, 0)),
                  pl.BlockSpec((d, tn), lambda i, j: (0, j)),
                  pl.BlockSpec((tm, tn), lambda i, j: (i, j)),
                  pl.BlockSpec((1, d), lambda i, j: (0, 0)),
                  pl.BlockSpec((d, ne), lambda i, j: (0, 0)),
                  pl.BlockSpec((1, ne), lambda i, j: (0, 0))],
        out_specs=[pl.BlockSpec((tm, tn), lambda i, j: (i, j)),
                   pl.BlockSpec((tm * (d // 2 // LANES), LANES), lambda i, j: (i, 0)),
                   pl.BlockSpec((tm, ne), lambda i, j: (i, 0))],
        out_shape=[jax.ShapeDtypeStruct((m, d), F32),
                   jax.ShapeDtypeStruct((m * (d // 2 // LANES), LANES), jnp.uint32),
                   jax.ShapeDtypeStruct((m, ne), F32)],
        scratch_shapes=[pltpu.VMEM((tm, d), F32)],
        compiler_params=_cparams(("parallel", "arbitrary")),
        name="out_proj",
    )(merged, w_out, x, gain, w_router, b_router)


def _gather_kernel(tok_ref, va_ref, x_hbm, o_ref, buf, sem, *, rows, nseg):
    i = pl.program_id(0)
    last = pl.num_programs(0) - 1

    def issue(blk, slot):
        def start(r, carry):
            t = tok_ref[blk * rows + r]
            pltpu.make_async_copy(x_hbm.at[pl.ds(pl.multiple_of(t * nseg, nseg), nseg), :],
                                  buf.at[pl.ds(pl.multiple_of((slot * rows + r) * nseg, nseg), nseg), :],
                                  sem.at[slot]).start()
            return carry
        lax.fori_loop(0, rows, start, 0, unroll=8)

    @pl.when(jnp.logical_and(i == 0, va_ref[0] == 1))
    def _():
        issue(0, 0)

    nxt = jnp.minimum(i + 1, last)

    @pl.when(jnp.logical_and(i < last, va_ref[nxt] == 1))
    def _():
        issue(nxt, nxt % 2)

    slot = i % 2

    @pl.when(va_ref[i] == 1)
    def _():
        base = pl.multiple_of(slot * rows * nseg, rows * nseg)
        pltpu.make_async_copy(x_hbm.at[pl.ds(0, rows * nseg), :], buf.at[pl.ds(base, rows * nseg), :],
                              sem.at[slot]).wait()
        half = nseg * LANES
        for s in range(nseg):
            u = buf[pl.ds(base + s, rows, stride=nseg), :]
            o_ref[:, s * LANES:(s + 1) * LANES] = lax.bitcast_convert_type(u << 16, F32).astype(o_ref.dtype)
            o_ref[:, half + s * LANES:half + (s + 1) * LANES] = (
                lax.bitcast_convert_type(u & jnp.uint32(0xFFFF0000), F32).astype(o_ref.dtype))

    @pl.when(va_ref[i] == 0)
    def _():
        o_ref[...] = jnp.zeros(o_ref.shape, o_ref.dtype)


def _gather_rows(row_token, valid, x, rows, nseg):
    n_rows = row_token.shape[0]
    d = 2 * nseg * LANES
    assert x.shape[0] >= rows * nseg and x.shape[1] == LANES
    return pl.pallas_call(
        functools.partial(_gather_kernel, rows=rows, nseg=nseg),
        grid_spec=pltpu.PrefetchScalarGridSpec(
            num_scalar_prefetch=2,
            grid=(n_rows // rows,),
            in_specs=[pl.BlockSpec(memory_space=pl.ANY)],
            out_specs=pl.BlockSpec((rows, d), lambda i, tok, va: (i, 0)),
            scratch_shapes=[pltpu.VMEM((2 * rows * nseg, LANES), x.dtype), pltpu.SemaphoreType.DMA((2,))]),
        out_shape=jax.ShapeDtypeStruct((n_rows, d), BF16),
        compiler_params=_cparams(("arbitrary",)),
        name="moe_gather",
    )(row_token, valid, x)


WEIGHT_COPY_SPLIT = 8


class _WeightCopy:
    def __init__(self, w_hbm, e, col, width, stage, sem):
        self.w_hbm, self.e, self.col, self.width, self.stage, self.sem = w_hbm, e, col, width, stage, sem

    def start(self):
        rows = self.stage.shape[0] // WEIGHT_COPY_SPLIT
        cols = pl.ds(pl.multiple_of(self.col, LANES), self.width)
        for c in range(WEIGHT_COPY_SPLIT):
            pltpu.make_async_copy(self.w_hbm.at[self.e, pl.ds(c * rows, rows), cols],
                                  self.stage.at[pl.ds(c * rows, rows), :], self.sem).start()

    def wait(self):
        pltpu.make_async_copy(self.w_hbm.at[0, :, pl.ds(0, self.width)], self.stage, self.sem).wait()


def _weight_copy(w_hbm, e, col, width, stage, sem):
    return _WeightCopy(w_hbm, e, col, width, stage, sem)


def _stream_weights(ne_ref, nx_ref, lr_ref, be_ref, ri_ref, copies):
    j = pl.program_id(0)
    i = pl.program_id(1)
    slot = (j * ri_ref[ri_ref.shape[0] - 1] + ri_ref[i]) % 2

    @pl.when(ne_ref[i] == 1)
    def _():
        cur = copies(be_ref[i], j, slot)

        @pl.when(jnp.logical_and(j == 0, i == 0))
        def _():
            for c in cur:
                c.start()

        for c in cur:
            c.wait()
        jn = jnp.where(lr_ref[i] == 1, j + 1, j)

        @pl.when(jn < pl.num_programs(0))
        def _():
            for c in copies(nx_ref[i], jn, 1 - slot):
                c.start()

    return slot


def _moe_up_kernel(be_ref, ne_ref, va_ref, nx_ref, lr_ref, ri_ref, x_ref, w_hbm, bg_ref, bu_ref, o_ref,
                   sgu, sem, *, tf, f):
    i = pl.program_id(1)

    def copies(e, jj, slot):
        return (_weight_copy(w_hbm, e, jj * tf, tf, sgu.at[slot, :, pl.ds(0, tf)], sem.at[0, slot]),
                _weight_copy(w_hbm, e, f + jj * tf, tf, sgu.at[slot, :, pl.ds(tf, tf)], sem.at[1, slot]))

    slot = _stream_weights(ne_ref, nx_ref, lr_ref, be_ref, ri_ref, copies)

    @pl.when(va_ref[i] == 1)
    def _():
        gu = _dot(x_ref[...], sgu[slot].astype(BF16))
        g = jnp.minimum(gu[:, :tf] + bg_ref[...], SWIGLU_LIMIT)
        u = jnp.clip(gu[:, tf:] + bu_ref[...], -SWIGLU_LIMIT, SWIGLU_LIMIT)
        o_ref[...] = ((u + 1.0) * g * _sigmoid(SWIGLU_ALPHA * g)).astype(o_ref.dtype)

    @pl.when(va_ref[i] == 0)
    def _():
        o_ref[...] = jnp.zeros(o_ref.shape, o_ref.dtype)


def _moe_up(sched, xs, w_gate_up, b_gate_up, rows, tf):
    n_rows, d = xs.shape
    f = w_gate_up.shape[2] // 2
    nf = f // tf
    return pl.pallas_call(
        functools.partial(_moe_up_kernel, tf=tf, f=f),
        grid_spec=pltpu.PrefetchScalarGridSpec(
            num_scalar_prefetch=6,
            grid=(nf, n_rows // rows),
            in_specs=[pl.BlockSpec((rows, d), lambda j, i, be, *_: (i, 0)),
                      pl.BlockSpec(memory_space=pl.ANY),
                      pl.BlockSpec((None, 1, tf), lambda j, i, be, *_: (be[i], 0, j)),
                      pl.BlockSpec((None, 1, tf), lambda j, i, be, *_: (be[i], 0, nf + j))],
            out_specs=pl.BlockSpec((rows, tf), lambda j, i, be, *_: (i, j)),
            scratch_shapes=[pltpu.VMEM((2, d, 2 * tf), F32), pltpu.SemaphoreType.DMA((2, 2))]),
        out_shape=jax.ShapeDtypeStruct((n_rows, f), BF16),
        compiler_params=_cparams(("arbitrary", "arbitrary")),
        name="moe_up",
    )(*sched, xs, w_gate_up, b_gate_up, b_gate_up)


def _moe_down_kernel(be_ref, ne_ref, va_ref, nx_ref, lr_ref, ri_ref, a_ref, w_hbm, b_ref, o_ref, sw, sem, *, tn):
    i = pl.program_id(1)

    def copies(e, jj, slot):
        return (_weight_copy(w_hbm, e, jj * tn, tn, sw.at[slot], sem.at[slot]),)

    slot = _stream_weights(ne_ref, nx_ref, lr_ref, be_ref, ri_ref, copies)

    @pl.when(va_ref[i] == 1)
    def _():
        o_ref[...] = _dot(a_ref[...], sw[slot].astype(BF16)) + b_ref[...]

    @pl.when(va_ref[i] == 0)
    def _():
        o_ref[...] = jnp.zeros(o_ref.shape, o_ref.dtype)


def _moe_down(sched, act, w_down, b_down, rows, tn):
    n_rows, f = act.shape
    d = w_down.shape[2]
    return pl.pallas_call(
        functools.partial(_moe_down_kernel, tn=tn),
        grid_spec=pltpu.PrefetchScalarGridSpec(
            num_scalar_prefetch=6,
            grid=(d // tn, n_rows // rows),
            in_specs=[pl.BlockSpec((rows, f), lambda j, i, be, *_: (i, 0)),
                      pl.BlockSpec(memory_space=pl.ANY),
                      pl.BlockSpec((None, 1, tn), lambda j, i, be, *_: (be[i], 0, j))],
            out_specs=pl.BlockSpec((rows, tn), lambda j, i, be, *_: (i, j)),
            scratch_shapes=[pltpu.VMEM((2, f, tn), F32), pltpu.SemaphoreType.DMA((2,))]),
        out_shape=jax.ShapeDtypeStruct((n_rows, d), F32),
        compiler_params=_cparams(("arbitrary", "arbitrary")),
        name="moe_down",
    )(*sched, act, w_down, b_down)


def _combine_kernel(pos_ref, h_ref, gt_ref, g_ref, y_hbm, h2_ref, hn_ref, buf, sem, *, tm, top_k):
    i = pl.program_id(0)
    last = pl.num_programs(0) - 1
    n = tm * top_k

    def issue(blk, slot):
        def start(a, carry):
            src = pos_ref[blk * n + a]
            pltpu.make_async_copy(y_hbm.at[pl.ds(src, 1), :],
                                  buf.at[slot, a % top_k, pl.ds(a // top_k, 1), :], sem.at[slot]).start()
            return carry
        lax.fori_loop(0, n, start, 0, unroll=8)

    @pl.when(i == 0)
    def _():
        issue(0, 0)

    nxt = jnp.minimum(i + 1, last)

    @pl.when(i < last)
    def _():
        issue(nxt, nxt % 2)

    slot = i % 2
    for k in range(top_k):
        pltpu.make_async_copy(y_hbm.at[pl.ds(0, tm), :], buf.at[slot, k], sem.at[slot]).wait()
    h = h_ref[...]
    gt = gt_ref[...]
    for k in range(top_k):
        h = h + gt[:, k:k + 1] * buf[slot, k]
    h2_ref[...] = h
    ms = jnp.mean(h * h, axis=-1, keepdims=True)
    hn_ref[...] = (h * lax.rsqrt(ms + RMS_EPS) * g_ref[...]).astype(hn_ref.dtype)


def _combine(pos, h, gates, gain, y_rows, tm):
    m, d = h.shape
    assert y_rows.shape[0] >= tm
    return pl.pallas_call(
        functools.partial(_combine_kernel, tm=tm, top_k=TOP_K),
        grid_spec=pltpu.PrefetchScalarGridSpec(
            num_scalar_prefetch=1,
            grid=(m // tm,),
            in_specs=[pl.BlockSpec((tm, d), lambda i, pos: (i, 0)),
                      pl.BlockSpec((tm, TOP_K), lambda i, pos: (i, 0)),
                      pl.BlockSpec((1, d), lambda i, pos: (0, 0)),
                      pl.BlockSpec(memory_space=pl.ANY)],
            out_specs=[pl.BlockSpec((tm, d), lambda i, pos: (i, 0)),
                       pl.BlockSpec((tm, d), lambda i, pos: (i, 0))],
            scratch_shapes=[pltpu.VMEM((2, TOP_K, tm, d), F32), pltpu.SemaphoreType.DMA((2,))]),
        out_shape=[jax.ShapeDtypeStruct((m, d), F32), jax.ShapeDtypeStruct((m, d), BF16)],
        compiler_params=_cparams(("arbitrary",)),
        name="moe_combine",
    )(pos, h, gates, gain, y_rows)


def _ple_kernel(hn_ref, wg_ref, p_ref, wp_ref, h_ref, o_ref):
    gate = _sigmoid(_dot(hn_ref[...], wg_ref[...]))
    o_ref[...] = h_ref[...] + gate * _dot(p_ref[...], wp_ref[...])


def _ple(hn, w_gate, p, w_proj, h, row0, n, tm, tn):
    d = h.shape[1]
    pd = p.shape[1]
    r0 = row0 // tm
    assert row0 % tm == 0 and n % tm == 0
    return pl.pallas_call(
        _ple_kernel,
        grid=(n // tm, d // tn),
        in_specs=[pl.BlockSpec((tm, d), lambda i, j: (r0 + i, 0)),
                  pl.BlockSpec((d, tn), lambda i, j: (0, j)),
                  pl.BlockSpec((tm, pd), lambda i, j: (i, 0)),
                  pl.BlockSpec((pd, tn), lambda i, j: (0, j)),
                  pl.BlockSpec((tm, tn), lambda i, j: (r0 + i, j))],
        out_specs=pl.BlockSpec((tm, tn), lambda i, j: (i, j)),
        out_shape=jax.ShapeDtypeStruct((n, d), F32),
        compiler_params=_cparams(("parallel", "arbitrary")),
        name="ple",
    )(hn, w_gate, p, w_proj, h)


def _route(logits, n_experts, rows):
    n = logits.shape[0]
    n_assign = n * TOP_K
    n_blocks = (n_assign + n_experts * (rows - 1) + rows - 1) // rows
    n_rows = n_blocks * rows
    top_logits, top_idx = lax.top_k(logits, TOP_K)
    gates = jax.nn.softmax(top_logits, axis=-1)
    expert = top_idx.reshape(-1).astype(jnp.int32)
    token = jnp.arange(n_assign, dtype=jnp.int32) // TOP_K
    onehot = expert[:, None] == jnp.arange(n_experts, dtype=jnp.int32)[None, :]
    cb = _divisor(n_assign, 256, 8)
    oh = onehot.reshape(n_assign // cb, cb, n_experts).astype(BF16)
    tri = (jnp.arange(cb)[:, None] >= jnp.arange(cb)[None, :]).astype(BF16)
    within = jnp.einsum('ij,bjk->bik', tri, oh, preferred_element_type=F32)
    totals = within[:, -1, :]
    offsets = jnp.cumsum(totals, axis=0) - totals
    csum = (within + offsets[:, None, :]).reshape(n_assign, n_experts)
    rank = jnp.sum(jnp.where(onehot, csum, 0.0), axis=1).astype(jnp.int32) - 1
    counts = (offsets[-1] + totals[-1]).astype(jnp.int32)
    padded = (counts + rows - 1) // rows * rows
    pend = jnp.cumsum(padded)
    pos = (pend - padded)[expert] + rank
    row_token = jnp.zeros((n_rows,), jnp.int32).at[pos].set(token)
    blk_start = jnp.arange(n_blocks, dtype=jnp.int32) * rows
    block_expert = jnp.minimum(jnp.sum((blk_start[:, None] >= pend[None, :]).astype(jnp.int32), axis=1), n_experts - 1)
    valid = (blk_start < pend[-1]).astype(jnp.int32)
    last_valid = block_expert[jnp.maximum(pend[-1] // rows - 1, 0)]
    block_expert = jnp.where(valid == 1, block_expert, last_valid)
    prev = jnp.concatenate([jnp.full((1,), -1, jnp.int32), block_expert[:-1]])
    new_expert = (block_expert != prev).astype(jnp.int32)
    run_index = jnp.cumsum(new_expert) - 1
    run_info = jnp.concatenate([run_index, run_index[-1:] + 1]).astype(jnp.int32)
    after = jnp.sum((block_expert[None, :] <= block_expert[:, None]).astype(jnp.int32), axis=1)
    last_run = (after >= n_blocks).astype(jnp.int32)
    next_expert = block_expert[jnp.where(after >= n_blocks, 0, after)]
    sched = (block_expert, new_expert, valid, next_expert, last_run, run_info)
    return pos.astype(jnp.int32), gates, row_token, sched


def kernel(x_prompt, x_sample, cache_k, cache_v, cache_logf, state_hgrn, page_table, p_prompt, p_sample,
           attn_norm, w_in, fox_fgate_bias, q_norm, k_norm, hg_lower_bound, hg_out_norm, w_proj_fox,
           w_proj_hg, w_out, ffn_norm, w_router, b_router, w_gate_up, b_gate_up, w_down, b_down,
           ple_norm, w_ple_gate, w_ple_proj):
    nb, t, d = x_prompt.shape
    nbs = x_sample.shape[0]
    assert x_sample.shape[1] == 1 and cache_k.shape[0] == 1
    _, _, page, nh, dh = cache_k.shape
    assert dh == LANES
    fw = nh * dh
    nhg = state_hgrn.shape[2]
    hw = nhg * LANES
    assert state_hgrn.shape[3] == LANES and state_hgrn.shape[4] == LANES and hw == fw
    n_experts = w_router.shape[2]
    mp = nb * t
    m = mp + nbs

    tm = _divisor(m, 704, 16)
    tn = _divisor(fw, 512, LANES)
    tq = _divisor(t, 512, LANES)
    chunk = LANES
    group = 8
    ts = _divisor(nbs, 128, 16)
    tp = _divisor(mp, 512, 16)
    hp = 4 if nhg % 4 == 0 and (3 * fw // LANES) % 4 == 0 else (2 if nhg % 2 == 0 else 1)
    pps = _divisor(page_table.shape[1], 4, 1)
    moe_rows = 256 if m * TOP_K // n_experts >= 512 else 16
    tm_out = tm
    tf = _divisor(d, 512, LANES)
    tdn = _divisor(d, 1024, LANES)
    assert nbs % group == 0 and mp % group == 0 and t % chunk == 0

    x = jnp.concatenate([x_prompt.reshape(mp, d), x_sample.reshape(nbs, d)], axis=0)
    w = w_in[0]
    ff0 = 3 * fw
    w_al = jnp.concatenate([w[:, :ff0], w[:, ff0 + nh:],
                            jnp.pad(w[:, ff0:ff0 + nh], ((0, 0), (0, tn - nh)))], axis=1).astype(BF16)
    blk = lambda col: col // LANES
    c_hq, c_hf, c_hi, c_hg = ff0, ff0 + hw, ff0 + 2 * hw, ff0 + 3 * hw
    c_ga = ff0 + 4 * hw
    c_gb = c_ga + d
    c_ff = c_gb + d
    lb = jax.nn.softmax(hg_lower_bound.astype(F32), axis=0)[0:1]
    fbias = jnp.pad(fox_fgate_bias, ((0, 0), (0, LANES - nh)))

    proj = _inproj(x, attn_norm, w_al, tm, tn)
    qn_p, kn_p, vn_p, logf_p = _prep(proj, q_norm, k_norm, fbias, 0, mp, tp, fw, blk(c_ff), LANES)
    qn_s, kn_s, vn_s, logf_s = _prep(proj, q_norm, k_norm, fbias, mp, nbs, ts, fw, blk(c_ff), LANES)

    c = _seq_cumsum(logf_p, nb, t)
    c_t = c[:, :nh].reshape(nb, t, nh).transpose(0, 2, 1).reshape(nb * nh, 1, t)
    o_fox_p = _fox_attn(qn_p, kn_p, vn_p, c, c_t, nb, t, nh, tq, 2 if nh % 2 == 0 else 1)

    lf_new = logf_s[:, :nh].reshape(nbs, 1, nh)
    bias = _page_bias(page_table, jnp.swapaxes(cache_logf, 2, 3), lf_new)
    bias = bias.reshape(nbs, page_table.shape[1], page * nh // LANES, LANES)
    o_fox_s = _paged_attn(page_table, qn_s.reshape(nbs, nh, dh), kn_s.reshape(nbs, nh, dh),
                          vn_s.reshape(nbs, nh, dh), cache_k, cache_v, bias, pps)

    o_hg_p, st_p = _hgrn_prompt(proj, lb, hg_out_norm, nb, t, nhg, blk(c_hq), blk(c_hf), blk(c_hi), blk(c_hg),
                                chunk, hp)
    o_hg_s, st_s = _hgrn_sample(proj, state_hgrn, lb, hg_out_norm, mp // group, nbs, nhg,
                                blk(c_hq), blk(c_hf), blk(c_hi), blk(c_hg), group)

    o_fox = jnp.concatenate([o_fox_p, o_fox_s.reshape(nbs, fw).astype(BF16)], axis=0)
    o_hg = jnp.concatenate([o_hg_p, o_hg_s.astype(BF16)], axis=0)
    merged = _merge(o_fox, o_hg, w_proj_fox[0].astype(BF16), w_proj_hg[0].astype(BF16), proj,
                    c_ga // tn, c_gb // tn, tm, tn)

    ne_pad = -(-n_experts // LANES) * LANES
    w_r = jnp.pad(w_router[0], ((0, 0), (0, ne_pad - n_experts))).astype(BF16)
    b_r = jnp.pad(b_router, ((0, 0), (0, ne_pad - n_experts)))
    h, hn, logits = _outproj(merged, w_out[0].astype(BF16), x, ffn_norm, w_r, b_r, tm_out, tn)

    pos, gates, row_token, sched = _route(logits[:, :n_experts], n_experts, moe_rows)
    xs = _gather_rows(row_token, sched[2], hn, moe_rows, d // 2 // LANES)
    act = _moe_up(sched, xs, w_gate_up.reshape(w_gate_up.shape[1:]), b_gate_up.reshape(n_experts, 1, -1),
                  moe_rows, tf)
    y_rows = _moe_down(sched, act, w_down.reshape(w_down.shape[1:]), b_down.reshape(n_experts, 1, -1),
                       moe_rows, tdn)
    h2, hn2 = _combine(pos, h, gates, ple_norm, y_rows, _divisor(m, 128, 16))

    w_pg = w_ple_gate[0].astype(BF16)
    w_pp = w_ple_proj[0].astype(BF16)
    y_p = _ple(hn2, w_pg, p_prompt[0].reshape(mp, -1).astype(BF16), w_pp, h2, 0, mp, tp, tn)
    y_s = _ple(hn2, w_pg, p_sample[0].reshape(nbs, -1).astype(BF16), w_pp, h2, mp, nbs, ts, tn)

    return (y_p.reshape(nb, t, d), y_s.reshape(nbs, 1, d),
            kn_p.reshape(1, nb, t, nh, dh), vn_p.reshape(1, nb, t, nh, dh), logf_p[:, :nh].reshape(1, nb, t, nh),
            st_p[None],
            kn_s.reshape(1, nbs, 1, nh, dh), vn_s.reshape(1, nbs, 1, nh, dh), logf_s[:, :nh].reshape(1, nbs, 1, nh),
            st_s)
```

```python
import functools

import jax
import jax.numpy as jnp
from jax import lax
from jax.experimental import pallas as pl
from jax.experimental.pallas import tpu as pltpu

F32 = jnp.float32
BF16 = jnp.bfloat16

TOP_K = 4
SWIGLU_LIMIT = 7.0
SWIGLU_ALPHA = 1.702
RMS_EPS = 1e-6
LANES = 128
MASK_VALUE = -1e30
VMEM_LIMIT = 56 * 1024 * 1024


def _cparams(sem):
    return pltpu.CompilerParams(dimension_semantics=sem, vmem_limit_bytes=VMEM_LIMIT)


def _divisor(n, cap, mult):
    best = None
    for d in range(mult, min(n, cap) + 1, mult):
        if n % d == 0:
            best = d
    assert best is not None, (n, cap, mult)
    return best


def _dot(a, b):
    return jnp.dot(a, b, preferred_element_type=F32)


def _dot_nt(a, b):
    return lax.dot_general(a, b, (((1,), (1,)), ((), ())), preferred_element_type=F32)


def _dot_tn(a, b):
    return lax.dot_general(a, b, (((0,), (0,)), ((), ())), preferred_element_type=F32)


def _split3(x):
    x1 = x.astype(BF16)
    r1 = x - x1.astype(F32)
    x2 = r1.astype(BF16)
    x3 = (r1 - x2.astype(F32)).astype(BF16)
    return x1, x2, x3


def _sel_dot(sel, x):
    x1, x2, x3 = _split3(x)
    return _dot(sel, x1) + _dot(sel, x2) + _dot(sel, x3)


def _sigmoid(x):
    return 1.0 / (1.0 + jnp.exp(-x))


def _iota(shape, dim):
    return lax.broadcasted_iota(jnp.int32, shape, dim)


def _inproj_kernel(x_ref, g_ref, w_ref, o_ref, xn_ref):
    @pl.when(pl.program_id(1) == 0)
    def _():
        x = x_ref[...]
        ms = jnp.mean(x * x, axis=-1, keepdims=True)
        xn_ref[...] = (x * lax.rsqrt(ms + RMS_EPS) * g_ref[...]).astype(BF16)

    o_ref[...] = _dot(xn_ref[...], w_ref[...])


def _inproj(x, gain, w, tm, tn):
    m, d = x.shape
    n = w.shape[1]
    return pl.pallas_call(
        _inproj_kernel,
        grid=(m // tm, n // tn),
        in_specs=[pl.BlockSpec((tm, d), lambda i, j: (i, 0)),
                  pl.BlockSpec((1, d), lambda i, j: (0, 0)),
                  pl.BlockSpec((d, tn), lambda i, j: (0, j))],
        out_specs=pl.BlockSpec((tm, tn), lambda i, j: (i, j)),
        out_shape=jax.ShapeDtypeStruct((m, n), F32),
        scratch_shapes=[pltpu.VMEM((tm, d), BF16)],
        compiler_params=_cparams(("parallel", "arbitrary")),
        name="inproj",
    )(x, gain, w)


def _prep_kernel(fq_ref, fk_ref, fv_ref, ff_ref, qg_ref, kg_ref, fb_ref, q_ref, k_ref, v_ref, lf_ref, *, nh):
    qg = qg_ref[...]
    kg = kg_ref[...]
    for h in range(nh):
        sl = slice(h * LANES, (h + 1) * LANES)
        for src, gain, dst in ((fq_ref, qg, q_ref), (fk_ref, kg, k_ref)):
            xh = src[:, sl]
            ms = jnp.mean(xh * xh, axis=-1, keepdims=True)
            dst[:, sl] = xh * lax.rsqrt(ms + RMS_EPS) * gain
    v_ref[...] = fv_ref[...]
    z = ff_ref[...] + fb_ref[...]
    lf_ref[...] = jnp.minimum(z, 0.0) - jnp.log1p(jnp.exp(-jnp.abs(z)))


def _prep(proj, q_gain, k_gain, fbias, row0, n, tm, fw, ff_blk, ffw):
    nh = fw // LANES
    r0 = row0 // tm
    assert row0 % tm == 0 and n % tm == 0
    return pl.pallas_call(
        functools.partial(_prep_kernel, nh=nh),
        grid=(n // tm,),
        in_specs=[pl.BlockSpec((tm, fw), lambda i: (r0 + i, 0)),
                  pl.BlockSpec((tm, fw), lambda i: (r0 + i, 1)),
                  pl.BlockSpec((tm, fw), lambda i: (r0 + i, 2)),
                  pl.BlockSpec((tm, ffw), lambda i: (r0 + i, ff_blk)),
                  pl.BlockSpec((1, LANES), lambda i: (0, 0)),
                  pl.BlockSpec((1, LANES), lambda i: (0, 0)),
                  pl.BlockSpec((1, ffw), lambda i: (0, 0))],
        out_specs=[pl.BlockSpec((tm, fw), lambda i: (i, 0)),
                   pl.BlockSpec((tm, fw), lambda i: (i, 0)),
                   pl.BlockSpec((tm, fw), lambda i: (i, 0)),
                   pl.BlockSpec((tm, ffw), lambda i: (i, 0))],
        out_shape=[jax.ShapeDtypeStruct((n, fw), F32),
                   jax.ShapeDtypeStruct((n, fw), F32),
                   jax.ShapeDtypeStruct((n, fw), F32),
                   jax.ShapeDtypeStruct((n, ffw), F32)],
        compiler_params=_cparams(("parallel",)),
        name="fox_prep",
    )(proj, proj, proj, proj, q_gain, k_gain, fbias)


def _cumsum_kernel(x_ref, o_ref, *, t, c):
    tri = (_iota((c, c), 0) >= _iota((c, c), 1)).astype(BF16)
    carry = jnp.zeros((1, x_ref.shape[1]), F32)
    for n in range(t // c):
        blk = _sel_dot(tri, x_ref[n * c:(n + 1) * c, :]) + carry
        o_ref[n * c:(n + 1) * c, :] = blk
        carry = blk[c - 1:c, :]


def _seq_cumsum(x, nb, t):
    w = x.shape[1]
    return pl.pallas_call(
        functools.partial(_cumsum_kernel, t=t, c=LANES),
        grid=(nb,),
        in_specs=[pl.BlockSpec((t, w), lambda b: (b, 0))],
        out_specs=pl.BlockSpec((t, w), lambda b: (b, 0)),
        out_shape=jax.ShapeDtypeStruct((nb * t, w), F32),
        compiler_params=_cparams(("parallel",)),
        name="logf_cumsum",
    )(x)


def _fox_attn_kernel(q_ref, k_ref, v_ref, cc_ref, cr_ref, o_ref, *, tq, scale, hpa):
    hg = pl.program_id(1)
    i = pl.program_id(2)
    lane = _iota(cc_ref.shape, 1)
    cc = cc_ref[...]
    causal = _iota((tq, tq), 0) >= _iota((tq, tq), 1)
    qs, cqs = [], []
    for hd in range(hpa):
        qs.append(q_ref[:, hd * LANES:(hd + 1) * LANES].astype(BF16))
        cqs.append(jnp.sum(jnp.where(lane == hg * hpa + hd, cc, 0.0), axis=-1, keepdims=True))

    def head_block(hd, r0, carry, diagonal):
        m, l, acc = carry
        sl = slice(hd * LANES, (hd + 1) * LANES)
        ks = k_ref[pl.ds(r0, tq), sl].astype(BF16)
        vs = v_ref[pl.ds(r0, tq), sl].astype(BF16)
        ck = cr_ref[hd, :, pl.ds(r0, tq)]
        s = _dot_nt(qs[hd], ks) * scale + (cqs[hd] - ck)
        if diagonal:
            s = jnp.where(causal, s, MASK_VALUE)
        m_new = jnp.maximum(m, jnp.max(s, axis=-1, keepdims=True))
        alpha = jnp.exp(m - m_new)
        p = jnp.exp(s - m_new)
        l = alpha * l + jnp.sum(p, axis=-1, keepdims=True)
        acc = alpha * acc + _dot(p.astype(BF16), vs)
        return m_new, l, acc

    def block(j, carries, diagonal):
        r0 = pl.multiple_of(j * tq, tq)
        return tuple(head_block(hd, r0, carries[hd], diagonal) for hd in range(hpa))

    init = tuple((jnp.full((tq, 1), MASK_VALUE, F32), jnp.zeros((tq, 1), F32), jnp.zeros((tq, LANES), F32))
                 for _ in range(hpa))
    carries = lax.fori_loop(0, i, lambda j, c: block(j, c, False), init)
    carries = block(i, carries, True)
    for hd in range(hpa):
        _, l, acc = carries[hd]
        o_ref[:, hd * LANES:(hd + 1) * LANES] = (acc / l).astype(o_ref.dtype)


def _fox_attn(qn, kn, vn, c, c_t, nb, t, nh, tq, hpa):
    nq = t // tq
    cw = c.shape[1]
    w = hpa * LANES
    return pl.pallas_call(
        functools.partial(_fox_attn_kernel, tq=tq, scale=LANES ** -0.5, hpa=hpa),
        grid=(nb, nh // hpa, nq),
        in_specs=[pl.BlockSpec((tq, w), lambda b, h, i: (b * nq + i, h)),
                  pl.BlockSpec((t, w), lambda b, h, i: (b, h)),
                  pl.BlockSpec((t, w), lambda b, h, i: (b, h)),
                  pl.BlockSpec((tq, cw), lambda b, h, i: (b * nq + i, 0)),
                  pl.BlockSpec((hpa, 1, t), lambda b, h, i: (b * (nh // hpa) + h, 0, 0))],
        out_specs=pl.BlockSpec((tq, w), lambda b, h, i: (b * nq + i, h)),
        out_shape=jax.ShapeDtypeStruct((nb * t, nh * LANES), BF16),
        compiler_params=_cparams(("parallel", "parallel", "arbitrary")),
        name="fox_attn",
    )(qn, kn, vn, c, c_t)


def _page_bias_kernel(pt_ref, *refs, n_pages, page):
    lf_refs = refs[:n_pages]
    new_ref = refs[n_pages]
    o_ref = refs[n_pages + 1]
    sel = ((_iota((page + 8, page), 0) < _iota((page + 8, page), 1)) | (_iota((page + 8, page), 0) >= page)).astype(BF16)
    carry = new_ref[...]
    for p in range(n_pages - 1, -1, -1):
        x1, x2, x3 = _split3(lf_refs[p][...])
        sums = _dot_nt(sel, x1) + _dot_nt(sel, x2) + _dot_nt(sel, x3)
        o_ref[p * page:(p + 1) * page, :] = sums[:page, :] + carry
        carry = carry + sums[page:page + 1, :]


def _page_bias(page_table, cache_logf_t, lf_new):
    nbs, n_pages = page_table.shape
    _, _, nh, page = cache_logf_t.shape
    in_specs = [pl.BlockSpec((None, None, nh, page),
                             functools.partial(lambda b, pt, p: (0, pt[b * n_pages + p], 0, 0), p=p))
                for p in range(n_pages)]
    in_specs.append(pl.BlockSpec((None, 1, nh), lambda b, pt: (b, 0, 0)))
    return pl.pallas_call(
        functools.partial(_page_bias_kernel, n_pages=n_pages, page=page),
        grid_spec=pltpu.PrefetchScalarGridSpec(
            num_scalar_prefetch=1,
            grid=(nbs,),
            in_specs=in_specs,
            out_specs=pl.BlockSpec((None, n_pages * page, nh), lambda b, pt: (b, 0, 0))),
        out_shape=jax.ShapeDtypeStruct((nbs, n_pages * page, nh), F32),
        compiler_params=_cparams(("arbitrary",)),
        name="page_bias",
    )(page_table.reshape(-1), *([cache_logf_t] * n_pages), lf_new)


def _paged_attn_kernel(pt_ref, q_ref, kn_ref, vn_ref, *refs, scale, page, nh, pps):
    k_refs, v_refs = refs[:pps], refs[pps:2 * pps]
    b_ref, o_ref, m_ref, l_ref, acc_ref = refs[2 * pps:]
    p = pl.program_id(1)
    q = q_ref[...]

    @pl.when(p == 0)
    def _():
        s_new = jnp.sum(q * kn_ref[...], axis=-1, keepdims=True) * scale
        m_ref[...] = jnp.broadcast_to(s_new, m_ref.shape)
        l_ref[...] = jnp.ones(l_ref.shape, F32)
        acc_ref[...] = vn_ref[...]

    rows = page * nh
    groups = rows // LANES
    eye = _iota((LANES, LANES), 0) == _iota((LANES, LANES), 1)
    ones = jnp.ones((2 * LANES, LANES), BF16)
    qs = q * scale
    scores = []
    for k_ref, g in zip(k_refs, range(pps)):
        kq = (k_ref[...] * qs[None]).reshape(groups, LANES, LANES)
        z = (kq + jnp.where(eye[None], b_ref[g][:, None, :], 0.0)).reshape(rows, LANES)
        hi = z.astype(BF16)
        lo = (z - hi.astype(F32)).astype(BF16)
        scores.append(_dot(jnp.concatenate([hi, lo], axis=1), ones).reshape(page, nh, LANES))
    m_old = m_ref[...]
    m_new = m_old
    for s in scores:
        m_new = jnp.maximum(m_new, jnp.max(s, axis=0))
    alpha = jnp.exp(m_old - m_new)
    l = alpha * l_ref[...]
    acc = alpha * acc_ref[...]
    for s, v_ref in zip(scores, v_refs):
        pr = jnp.exp(s - m_new[None])
        l = l + jnp.sum(pr, axis=0)
        acc = acc + jnp.sum(pr * v_ref[...], axis=0)
    l_ref[...] = l
    acc_ref[...] = acc
    m_ref[...] = m_new

    @pl.when(p == pl.num_programs(1) - 1)
    def _():
        o_ref[...] = acc_ref[...] / l_ref[...]


def _paged_attn(page_table, q, k_new, v_new, cache_k, cache_v, bias, pps):
    nbs, n_pages = page_table.shape
    _, _, page, nh, dh = cache_k.shape
    groups = page * nh // LANES
    tok = pl.BlockSpec((None, nh, dh), lambda b, p, pt: (b, 0, 0))
    pgs = [pl.BlockSpec((None, None, page, nh, dh),
                        functools.partial(lambda b, p, pt, g: (0, pt[b * n_pages + p * pps + g], 0, 0, 0), g=g))
           for g in range(pps)]
    return pl.pallas_call(
        functools.partial(_paged_attn_kernel, scale=dh ** -0.5, page=page, nh=nh, pps=pps),
        grid_spec=pltpu.PrefetchScalarGridSpec(
            num_scalar_prefetch=1,
            grid=(nbs, n_pages // pps),
            in_specs=[tok, tok, tok] + pgs + pgs +
                     [pl.BlockSpec((None, pps, groups, LANES), lambda b, p, pt: (b, p, 0, 0))],
            out_specs=tok,
            scratch_shapes=[pltpu.VMEM((nh, dh), F32)] * 3),
        out_shape=jax.ShapeDtypeStruct((nbs, nh, dh), F32),
        compiler_params=_cparams(("parallel", "arbitrary")),
        name="paged_attn",
    )(page_table.reshape(-1), q, k_new, v_new, *([cache_k] * pps), *([cache_v] * pps), bias)


def _hgrn_levels(c):
    levels = []
    m = 1
    while m < c:
        levels.append(m)
        m *= 2
    return levels


def _hgrn_prompt_kernel(hq_ref, hf_ref, hi_ref, hg_ref, lb_ref, gain_ref, o_ref, st_ref, *, t, c, hp):
    levels = _hgrn_levels(c)
    row = _iota((c, c), 0)
    col = _iota((c, c), 1)
    eye = row == col
    cums, masks = [(row >= col).astype(BF16)], []
    for m in levels:
        boundary = (row // (2 * m)) * (2 * m) + (m - 1)
        if 2 * m < 16:
            cums.append((col <= boundary).astype(BF16))
        same = (row // (2 * m)) == (col // (2 * m))
        masks.append(same & ((row % (2 * m)) >= m) & ((col % (2 * m)) < m))
    n_mm = len(cums) - 1
    cum_all = jnp.concatenate(cums, axis=0)
    gain = gain_ref[...]

    def head_chunk(r0, hd, s_t):
        sl = slice(hd * LANES, (hd + 1) * LANES)
        lb = lb_ref[:, sl]
        hq = hq_ref[pl.ds(r0, c), sl]
        fg = lb + (1.0 - lb) * _sigmoid(hf_ref[pl.ds(r0, c), sl])
        q = hq * _sigmoid(hq)
        k = 1.0 - fg
        v = hi_ref[pl.ds(r0, c), sl].astype(BF16)
        sums = _sel_dot(cum_all, jnp.log(fg))
        b = sums[:c, :]
        a = jnp.where(eye, _dot_nt(q.astype(BF16), k.astype(BF16)), 0.0)
        for li, m in enumerate(levels):
            if li < n_mm:
                rl = sums[(li + 1) * c:(li + 2) * c, :]
            else:
                b3 = b.reshape(c // (2 * m), 2 * m, LANES)
                rl = jnp.broadcast_to(b3[:, m - 1:m, :], b3.shape).reshape(c, LANES)
            ql = q * jnp.exp(jnp.minimum(b - rl, 0.0))
            kl = k * jnp.exp(jnp.minimum(rl - b, 0.0))
            a = a + jnp.where(masks[li], _dot_nt(ql.astype(BF16), kl.astype(BF16)), 0.0)
        o = _dot(a.astype(BF16), v) + _dot_nt((q * jnp.exp(b)).astype(BF16), s_t.astype(BF16))
        b_last = b[c - 1:c, :]
        kd = k * jnp.exp(b_last - b)
        s_t = s_t * jnp.exp(b_last) + _dot_tn(v, kd.astype(BF16))
        ms = jnp.mean(o * o, axis=-1, keepdims=True)
        hg = hg_ref[pl.ds(r0, c), sl]
        o_ref[pl.ds(r0, c), sl] = (o * lax.rsqrt(ms + RMS_EPS) * gain * (hg * _sigmoid(hg))).astype(o_ref.dtype)
        return s_t

    def chunk(n, states):
        r0 = pl.multiple_of(n * c, c)
        return tuple(head_chunk(r0, hd, states[hd]) for hd in range(hp))

    states = lax.fori_loop(0, t // c, chunk, tuple(jnp.zeros((LANES, LANES), F32) for _ in range(hp)))
    ident = (_iota((LANES, LANES), 0) == _iota((LANES, LANES), 1)).astype(BF16)
    for hd in range(hp):
        s1, s2, s3 = _split3(states[hd])
        st_ref[hd] = _dot_nt(ident, s1) + _dot_nt(ident, s2) + _dot_nt(ident, s3)


def _hgrn_prompt(proj, lb, gain, nb, t, nh, blk_q, blk_f, blk_i, blk_g, c, hp):
    w = hp * LANES

    def col(blk):
        assert blk % hp == 0
        return pl.BlockSpec((t, w), lambda b, h: (b, blk // hp + h))
    return pl.pallas_call(
        functools.partial(_hgrn_prompt_kernel, t=t, c=c, hp=hp),
        grid=(nb, nh // hp),
        in_specs=[col(blk_q), col(blk_f), col(blk_i), col(blk_g),
                  pl.BlockSpec((1, w), lambda b, h: (0, h)),
                  pl.BlockSpec((1, LANES), lambda b, h: (0, 0))],
        out_specs=[pl.BlockSpec((t, w), lambda b, h: (b, h)),
                   pl.BlockSpec((None, hp, LANES, LANES), lambda b, h: (b, h, 0, 0))],
        out_shape=[jax.ShapeDtypeStruct((nb * t, nh * LANES), BF16),
                   jax.ShapeDtypeStruct((nb, nh, LANES, LANES), F32)],
        compiler_params=_cparams(("parallel", "parallel")),
        name="hgrn_prompt",
    )(proj, proj, proj, proj, lb, gain)


def _hgrn_sample_kernel(hq_ref, hf_ref, hi_ref, hg_ref, lb_ref, gain_ref, s_ref, o_ref, so_ref, *, group):
    eye = _iota((LANES, LANES), 0) == _iota((LANES, LANES), 1)

    def column(rowvec):
        return jnp.sum(jnp.where(eye, jnp.broadcast_to(rowvec, (LANES, LANES)), 0.0), axis=-1, keepdims=True)

    lb = lb_ref[...]
    gain = gain_ref[...]
    for s in range(group):
        hq = hq_ref[s:s + 1, :]
        fg = lb + (1.0 - lb) * _sigmoid(hf_ref[s:s + 1, :])
        q = hq * _sigmoid(hq)
        s_new = column(fg) * s_ref[s] + column(1.0 - fg) * hi_ref[s:s + 1, :]
        so_ref[s] = s_new
        o = jnp.sum(column(q) * s_new, axis=0, keepdims=True)
        ms = jnp.mean(o * o, axis=-1, keepdims=True)
        hg = hg_ref[s:s + 1, :]
        o_ref[s:s + 1, :] = o * lax.rsqrt(ms + RMS_EPS) * gain * (hg * _sigmoid(hg))


def _hgrn_sample(proj, state, lb, gain, row_blk0, nbs, nh, blk_q, blk_f, blk_i, blk_g, group):
    def col(blk):
        return pl.BlockSpec((group, LANES), lambda h, i: (row_blk0 + i, blk + h))
    st = pl.BlockSpec((None, group, None, LANES, LANES), lambda h, i: (0, i, h, 0, 0))
    return pl.pallas_call(
        functools.partial(_hgrn_sample_kernel, group=group),
        grid=(nh, nbs // group),
        in_specs=[col(blk_q), col(blk_f), col(blk_i), col(blk_g),
                  pl.BlockSpec((1, LANES), lambda h, i: (0, h)),
                  pl.BlockSpec((1, LANES), lambda h, i: (0, 0)),
                  st],
        out_specs=[pl.BlockSpec((group, LANES), lambda h, i: (i, h)), st],
        out_shape=[jax.ShapeDtypeStruct((nbs, nh * LANES), F32),
                   jax.ShapeDtypeStruct((1,) + state.shape[1:], F32)],
        compiler_params=_cparams(("parallel", "parallel")),
        name="hgrn_sample",
    )(proj, proj, proj, proj, lb, gain, state)


def _merge_kernel(of_ref, oh_ref, wf_ref, wh_ref, ga_ref, gb_ref, o_ref):
    a = _dot(of_ref[...], wf_ref[...])
    b = _dot(oh_ref[...], wh_ref[...])
    o_ref[...] = (_sigmoid(ga_ref[...]) * a + _sigmoid(gb_ref[...]) * b).astype(o_ref.dtype)


def _merge(o_fox, o_hg, wf, wh, proj, blk_ga, blk_gb, tm, tn):
    m, kf = o_fox.shape
    n = wf.shape[1]
    return pl.pallas_call(
        _merge_kernel,
        grid=(m // tm, n // tn),
        in_specs=[pl.BlockSpec((tm, kf), lambda i, j: (i, 0)),
                  pl.BlockSpec((tm, kf), lambda i, j: (i, 0)),
                  pl.BlockSpec((kf, tn), lambda i, j: (0, j)),
                  pl.BlockSpec((kf, tn), lambda i, j: (0, j)),
                  pl.BlockSpec((tm, tn), lambda i, j: (i, blk_ga + j)),
                  pl.BlockSpec((tm, tn), lambda i, j: (i, blk_gb + j))],
        out_specs=pl.BlockSpec((tm, tn), lambda i, j: (i, j)),
        out_shape=jax.ShapeDtypeStruct((m, n), BF16),
        compiler_params=_cparams(("parallel", "arbitrary")),
        name="branch_merge",
    )(o_fox, o_hg, wf, wh, proj, proj)


def _outproj_kernel(mg_ref, w_ref, x_ref, g_ref, wr_ref, br_ref, h_ref, hn_ref, lg_ref, hrow_ref, *, tn):
    j = pl.program_id(1)
    hblk = x_ref[...] + _dot(mg_ref[...], w_ref[...])
    h_ref[...] = hblk
    c0 = pl.multiple_of(j * tn, tn)
    hrow_ref[:, pl.ds(c0, tn)] = hblk

    @pl.when(j == pl.num_programs(1) - 1)
    def _():
        h = hrow_ref[...]
        ms = jnp.mean(h * h, axis=-1, keepdims=True)
        hn = (h * lax.rsqrt(ms + RMS_EPS) * g_ref[...]).astype(BF16)
        lg_ref[...] = _dot(hn, wr_ref[...]) + br_ref[...]
        half = hn.shape[1] // 2
        lo = lax.bitcast_convert_type(hn[:, :half].astype(F32), jnp.uint32) >> 16
        hi = lax.bitcast_convert_type(hn[:, half:].astype(F32), jnp.uint32)
        word = hi | lo
        nseg = half // LANES
        for s in range(nseg):
            hn_ref[pl.ds(s, hn.shape[0], stride=nseg), :] = word[:, s * LANES:(s + 1) * LANES]


def _outproj(merged, w_out, x, gain, w_router, b_router, tm, tn):
    m, d = x.shape
    ne = w_router.shape[1]
    return pl.pallas_call(
        functools.partial(_outproj_kernel, tn=tn),
        grid=(m // tm, d // tn),
        in_specs=[pl.BlockSpec((tm, d), lambda i, j: (i, 0)),
                  pl.BlockSpec((d, tn), lambda i, j: (0, j)),
                  pl.BlockSpec((tm, tn), lambda i, j: (i, j)),
                  pl.BlockSpec((1, d), lambda i, j: (0, 0)),
                  pl.BlockSpec((d, ne), lambda i, j: (0, 0)),
                  pl.BlockSpec((1, ne), lambda i, j: (0, 0))],
        out_specs=[pl.BlockSpec((tm, tn), lambda i, j: (i, j)),
                   pl.BlockSpec((tm * (d // 2 // LANES), LANES), lambda i, j: (i, 0)),
                   pl.BlockSpec((tm, ne), lambda i, j: (i, 0))],
        out_shape=[jax.ShapeDtypeStruct((m, d), F32),
                   jax.ShapeDtypeStruct((m * (d // 2 // LANES), LANES), jnp.uint32),
                   jax.ShapeDtypeStruct((m, ne), F32)],
        scratch_shapes=[pltpu.VMEM((tm, d), F32)],
        compiler_params=_cparams(("parallel", "arbitrary")),
        name="out_proj",
    )(merged, w_out, x, gain, w_router, b_router)


def _gather_kernel(tok_ref, va_ref, x_hbm, o_ref, buf, sem, *, rows, nseg):
    i = pl.program_id(0)
    last = pl.num_programs(0) - 1

    def issue(blk, slot):
        def start(r, carry):
            t = tok_ref[blk * rows + r]
            pltpu.make_async_copy(x_hbm.at[pl.ds(pl.multiple_of(t * nseg, nseg), nseg), :],
                                  buf.at[pl.ds(pl.multiple_of((slot * rows + r) * nseg, nseg), nseg), :],
                                  sem.at[slot]).start()
            return carry
        lax.fori_loop(0, rows, start, 0, unroll=8)

    @pl.when(jnp.logical_and(i == 0, va_ref[0] == 1))
    def _():
        issue(0, 0)

    nxt = jnp.minimum(i + 1, last)

    @pl.when(jnp.logical_and(i < last, va_ref[nxt] == 1))
    def _():
        issue(nxt, nxt % 2)

    slot = i % 2

    @pl.when(va_ref[i] == 1)
    def _():
        base = pl.multiple_of(slot * rows * nseg, rows * nseg)
        pltpu.make_async_copy(x_hbm.at[pl.ds(0, rows * nseg), :], buf.at[pl.ds(base, rows * nseg), :],
                              sem.at[slot]).wait()
        half = nseg * LANES
        for s in range(nseg):
            u = buf[pl.ds(base + s, rows, stride=nseg), :]
            o_ref[:, s * LANES:(s + 1) * LANES] = lax.bitcast_convert_type(u << 16, F32).astype(o_ref.dtype)
            o_ref[:, half + s * LANES:half + (s + 1) * LANES] = (
                lax.bitcast_convert_type(u & jnp.uint32(0xFFFF0000), F32).astype(o_ref.dtype))

    @pl.when(va_ref[i] == 0)
    def _():
        o_ref[...] = jnp.zeros(o_ref.shape, o_ref.dtype)


def _gather_rows(row_token, valid, x, rows, nseg):
    n_rows = row_token.shape[0]
    d = 2 * nseg * LANES
    assert x.shape[0] >= rows * nseg and x.shape[1] == LANES
    return pl.pallas_call(
        functools.partial(_gather_kernel, rows=rows, nseg=nseg),
        grid_spec=pltpu.PrefetchScalarGridSpec(
            num_scalar_prefetch=2,
            grid=(n_rows // rows,),
            in_specs=[pl.BlockSpec(memory_space=pl.ANY)],
            out_specs=pl.BlockSpec((rows, d), lambda i, tok, va: (i, 0)),
            scratch_shapes=[pltpu.VMEM((2 * rows * nseg, LANES), x.dtype), pltpu.SemaphoreType.DMA((2,))]),
        out_shape=jax.ShapeDtypeStruct((n_rows, d), BF16),
        compiler_params=_cparams(("arbitrary",)),
        name="moe_gather",
    )(row_token, valid, x)


WEIGHT_COPY_SPLIT = 8


class _WeightCopy:
    def __init__(self, w_hbm, e, col, width, stage, sem):
        self.w_hbm, self.e, self.col, self.width, self.stage, self.sem = w_hbm, e, col, width, stage, sem

    def start(self):
        rows = self.stage.shape[0] // WEIGHT_COPY_SPLIT
        cols = pl.ds(pl.multiple_of(self.col, LANES), self.width)
        for c in range(WEIGHT_COPY_SPLIT):
            pltpu.make_async_copy(self.w_hbm.at[self.e, pl.ds(c * rows, rows), cols],
                                  self.stage.at[pl.ds(c * rows, rows), :], self.sem).start()

    def wait(self):
        pltpu.make_async_copy(self.w_hbm.at[0, :, pl.ds(0, self.width)], self.stage, self.sem).wait()


def _weight_copy(w_hbm, e, col, width, stage, sem):
    return _WeightCopy(w_hbm, e, col, width, stage, sem)


def _stream_weights(ne_ref, nx_ref, lr_ref, be_ref, ri_ref, copies):
    j = pl.program_id(0)
    i = pl.program_id(1)
    slot = (j * ri_ref[ri_ref.shape[0] - 1] + ri_ref[i]) % 2

    @pl.when(ne_ref[i] == 1)
    def _():
        cur = copies(be_ref[i], j, slot)

        @pl.when(jnp.logical_and(j == 0, i == 0))
        def _():
            for c in cur:
                c.start()

        for c in cur:
            c.wait()
        jn = jnp.where(lr_ref[i] == 1, j + 1, j)

        @pl.when(jn < pl.num_programs(0))
        def _():
            for c in copies(nx_ref[i], jn, 1 - slot):
                c.start()

    return slot


def _moe_up_kernel(be_ref, ne_ref, va_ref, nx_ref, lr_ref, ri_ref, x_ref, w_hbm, bg_ref, bu_ref, o_ref,
                   sgu, sem, *, tf, f):
    i = pl.program_id(1)

    def copies(e, jj, slot):
        return (_weight_copy(w_hbm, e, jj * tf, tf, sgu.at[slot, :, pl.ds(0, tf)], sem.at[0, slot]),
                _weight_copy(w_hbm, e, f + jj * tf, tf, sgu.at[slot, :, pl.ds(tf, tf)], sem.at[1, slot]))

    slot = _stream_weights(ne_ref, nx_ref, lr_ref, be_ref, ri_ref, copies)

    half = x_ref.shape[0] // 2
    filled = va_ref[i]

    def swiglu(x):
        gu = _dot(x, sgu[slot].astype(BF16))
        g = jnp.minimum(gu[:, :tf] + bg_ref[...], SWIGLU_LIMIT)
        u = jnp.clip(gu[:, tf:] + bu_ref[...], -SWIGLU_LIMIT, SWIGLU_LIMIT)
        return ((u + 1.0) * g * _sigmoid(SWIGLU_ALPHA * g)).astype(o_ref.dtype)

    @pl.when(filled > half)
    def _():
        o_ref[...] = swiglu(x_ref[...])

    @pl.when(jnp.logical_and(filled > 0, filled <= half))
    def _():
        o_ref[:half, :] = swiglu(x_ref[:half, :])
        o_ref[half:, :] = jnp.zeros((o_ref.shape[0] - half, o_ref.shape[1]), o_ref.dtype)

    @pl.when(filled == 0)
    def _():
        o_ref[...] = jnp.zeros(o_ref.shape, o_ref.dtype)


def _moe_up(sched, xs, w_gate_up, b_gate_up, rows, tf):
    n_rows, d = xs.shape
    f = w_gate_up.shape[2] // 2
    nf = f // tf
    return pl.pallas_call(
        functools.partial(_moe_up_kernel, tf=tf, f=f),
        grid_spec=pltpu.PrefetchScalarGridSpec(
            num_scalar_prefetch=6,
            grid=(nf, n_rows // rows),
            in_specs=[pl.BlockSpec((rows, d), lambda j, i, be, *_: (i, 0)),
                      pl.BlockSpec(memory_space=pl.ANY),
                      pl.BlockSpec((None, 1, tf), lambda j, i, be, *_: (be[i], 0, j)),
                      pl.BlockSpec((None, 1, tf), lambda j, i, be, *_: (be[i], 0, nf + j))],
            out_specs=pl.BlockSpec((rows, tf), lambda j, i, be, *_: (i, j)),
            scratch_shapes=[pltpu.VMEM((2, d, 2 * tf), F32), pltpu.SemaphoreType.DMA((2, 2))]),
        out_shape=jax.ShapeDtypeStruct((n_rows, f), BF16),
        compiler_params=_cparams(("arbitrary", "arbitrary")),
        name="moe_up",
    )(*sched, xs, w_gate_up, b_gate_up, b_gate_up)


def _moe_down_kernel(be_ref, ne_ref, va_ref, nx_ref, lr_ref, ri_ref, a_ref, w_hbm, b_ref, o_ref, sw, sem, *, tn):
    i = pl.program_id(1)

    def copies(e, jj, slot):
        return (_weight_copy(w_hbm, e, jj * tn, tn, sw.at[slot], sem.at[slot]),)

    slot = _stream_weights(ne_ref, nx_ref, lr_ref, be_ref, ri_ref, copies)

    half = a_ref.shape[0] // 2
    filled = va_ref[i]

    @pl.when(filled > half)
    def _():
        o_ref[...] = _dot(a_ref[...], sw[slot].astype(BF16)) + b_ref[...]

    @pl.when(jnp.logical_and(filled > 0, filled <= half))
    def _():
        o_ref[:half, :] = _dot(a_ref[:half, :], sw[slot].astype(BF16)) + b_ref[...]
        o_ref[half:, :] = jnp.zeros((o_ref.shape[0] - half, o_ref.shape[1]), o_ref.dtype)

    @pl.when(filled == 0)
    def _():
        o_ref[...] = jnp.zeros(o_ref.shape, o_ref.dtype)


def _moe_down(sched, act, w_down, b_down, rows, tn):
    n_rows, f = act.shape
    d = w_down.shape[2]
    return pl.pallas_call(
        functools.partial(_moe_down_kernel, tn=tn),
        grid_spec=pltpu.PrefetchScalarGridSpec(
            num_scalar_prefetch=6,
            grid=(d // tn, n_rows // rows),
            in_specs=[pl.BlockSpec((rows, f), lambda j, i, be, *_: (i, 0)),
                      pl.BlockSpec(memory_space=pl.ANY),
                      pl.BlockSpec((None, 1, tn), lambda j, i, be, *_: (be[i], 0, j))],
            out_specs=pl.BlockSpec((rows, tn), lambda j, i, be, *_: (i, j)),
            scratch_shapes=[pltpu.VMEM((2, f, tn), F32), pltpu.SemaphoreType.DMA((2,))]),
        out_shape=jax.ShapeDtypeStruct((n_rows, d), F32),
        compiler_params=_cparams(("arbitrary", "arbitrary")),
        name="moe_down",
    )(*sched, act, w_down, b_down)


def _combine_kernel(pos_ref, h_ref, gt_ref, g_ref, y_hbm, h2_ref, hn_ref, buf, sem, *, tm, top_k):
    i = pl.program_id(0)
    last = pl.num_programs(0) - 1
    n = tm * top_k

    def issue(blk, slot):
        def start(a, carry):
            src = pos_ref[blk * n + a]
            pltpu.make_async_copy(y_hbm.at[pl.ds(src, 1), :],
                                  buf.at[slot, a % top_k, pl.ds(a // top_k, 1), :], sem.at[slot]).start()
            return carry
        lax.fori_loop(0, n, start, 0, unroll=8)

    @pl.when(i == 0)
    def _():
        issue(0, 0)

    nxt = jnp.minimum(i + 1, last)

    @pl.when(i < last)
    def _():
        issue(nxt, nxt % 2)

    slot = i % 2
    for k in range(top_k):
        pltpu.make_async_copy(y_hbm.at[pl.ds(0, tm), :], buf.at[slot, k], sem.at[slot]).wait()
    h = h_ref[...]
    gt = gt_ref[...]
    for k in range(top_k):
        h = h + gt[:, k:k + 1] * buf[slot, k]
    h2_ref[...] = h
    ms = jnp.mean(h * h, axis=-1, keepdims=True)
    hn_ref[...] = (h * lax.rsqrt(ms + RMS_EPS) * g_ref[...]).astype(hn_ref.dtype)


def _combine(pos, h, gates, gain, y_rows, tm):
    m, d = h.shape
    assert y_rows.shape[0] >= tm
    return pl.pallas_call(
        functools.partial(_combine_kernel, tm=tm, top_k=TOP_K),
        grid_spec=pltpu.PrefetchScalarGridSpec(
            num_scalar_prefetch=1,
            grid=(m // tm,),
            in_specs=[pl.BlockSpec((tm, d), lambda i, pos: (i, 0)),
                      pl.BlockSpec((tm, TOP_K), lambda i, pos: (i, 0)),
                      pl.BlockSpec((1, d), lambda i, pos: (0, 0)),
                      pl.BlockSpec(memory_space=pl.ANY)],
            out_specs=[pl.BlockSpec((tm, d), lambda i, pos: (i, 0)),
                       pl.BlockSpec((tm, d), lambda i, pos: (i, 0))],
            scratch_shapes=[pltpu.VMEM((2, TOP_K, tm, d), F32), pltpu.SemaphoreType.DMA((2,))]),
        out_shape=[jax.ShapeDtypeStruct((m, d), F32), jax.ShapeDtypeStruct((m, d), BF16)],
        compiler_params=_cparams(("arbitrary",)),
        name="moe_combine",
    )(pos, h, gates, gain, y_rows)


def _ple_kernel(hn_ref, wg_ref, p_ref, wp_ref, h_ref, o_ref):
    gate = _sigmoid(_dot(hn_ref[...], wg_ref[...]))
    o_ref[...] = h_ref[...] + gate * _dot(p_ref[...], wp_ref[...])


def _ple(hn, w_gate, p, w_proj, h, row0, n, tm, tn):
    d = h.shape[1]
    pd = p.shape[1]
    r0 = row0 // tm
    assert row0 % tm == 0 and n % tm == 0
    return pl.pallas_call(
        _ple_kernel,
        grid=(n // tm, d // tn),
        in_specs=[pl.BlockSpec((tm, d), lambda i, j: (r0 + i, 0)),
                  pl.BlockSpec((d, tn), lambda i, j: (0, j)),
                  pl.BlockSpec((tm, pd), lambda i, j: (i, 0)),
                  pl.BlockSpec((pd, tn), lambda i, j: (0, j)),
                  pl.BlockSpec((tm, tn), lambda i, j: (r0 + i, j))],
        out_specs=pl.BlockSpec((tm, tn), lambda i, j: (i, j)),
        out_shape=jax.ShapeDtypeStruct((n, d), F32),
        compiler_params=_cparams(("parallel", "arbitrary")),
        name="ple",
    )(hn, w_gate, p, w_proj, h)


def _route(logits, n_experts, rows):
    n = logits.shape[0]
    n_assign = n * TOP_K
    n_blocks = (n_assign + n_experts * (rows - 1) + rows - 1) // rows
    n_rows = n_blocks * rows
    top_logits, top_idx = lax.top_k(logits, TOP_K)
    gates = jax.nn.softmax(top_logits, axis=-1)
    expert = top_idx.reshape(-1).astype(jnp.int32)
    token = jnp.arange(n_assign, dtype=jnp.int32) // TOP_K
    onehot = expert[:, None] == jnp.arange(n_experts, dtype=jnp.int32)[None, :]
    cb = _divisor(n_assign, 256, 8)
    oh = onehot.reshape(n_assign // cb, cb, n_experts).astype(BF16)
    tri = (jnp.arange(cb)[:, None] >= jnp.arange(cb)[None, :]).astype(BF16)
    within = jnp.einsum('ij,bjk->bik', tri, oh, preferred_element_type=F32)
    totals = within[:, -1, :]
    offsets = jnp.cumsum(totals, axis=0) - totals
    csum = (within + offsets[:, None, :]).reshape(n_assign, n_experts)
    rank = jnp.sum(jnp.where(onehot, csum, 0.0), axis=1).astype(jnp.int32) - 1
    counts = (offsets[-1] + totals[-1]).astype(jnp.int32)
    padded = (counts + rows - 1) // rows * rows
    pend = jnp.cumsum(padded)
    pos = (pend - padded)[expert] + rank
    row_token = jnp.zeros((n_rows,), jnp.int32).at[pos].set(token)
    blk_start = jnp.arange(n_blocks, dtype=jnp.int32) * rows
    block_expert = jnp.minimum(jnp.sum((blk_start[:, None] >= pend[None, :]).astype(jnp.int32), axis=1), n_experts - 1)
    valid = (blk_start < pend[-1]).astype(jnp.int32)
    last_valid = block_expert[jnp.maximum(pend[-1] // rows - 1, 0)]
    block_expert = jnp.where(valid == 1, block_expert, last_valid)
    prev = jnp.concatenate([jnp.full((1,), -1, jnp.int32), block_expert[:-1]])
    new_expert = (block_expert != prev).astype(jnp.int32)
    run_index = jnp.cumsum(new_expert) - 1
    run_info = jnp.concatenate([run_index, run_index[-1:] + 1]).astype(jnp.int32)
    after = jnp.sum((block_expert[None, :] <= block_expert[:, None]).astype(jnp.int32), axis=1)
    last_run = (after >= n_blocks).astype(jnp.int32)
    next_expert = block_expert[jnp.where(after >= n_blocks, 0, after)]
    filled = jnp.clip((pend - padded + counts)[block_expert] - blk_start, 0, rows) * valid
    sched = (block_expert, new_expert, filled.astype(jnp.int32), next_expert, last_run, run_info)
    return pos.astype(jnp.int32), gates, row_token, sched


def kernel(x_prompt, x_sample, cache_k, cache_v, cache_logf, state_hgrn, page_table, p_prompt, p_sample,
           attn_norm, w_in, fox_fgate_bias, q_norm, k_norm, hg_lower_bound, hg_out_norm, w_proj_fox,
           w_proj_hg, w_out, ffn_norm, w_router, b_router, w_gate_up, b_gate_up, w_down, b_down,
           ple_norm, w_ple_gate, w_ple_proj):
    nb, t, d = x_prompt.shape
    nbs = x_sample.shape[0]
    assert x_sample.shape[1] == 1 and cache_k.shape[0] == 1
    _, _, page, nh, dh = cache_k.shape
    assert dh == LANES
    fw = nh * dh
    nhg = state_hgrn.shape[2]
    hw = nhg * LANES
    assert state_hgrn.shape[3] == LANES and state_hgrn.shape[4] == LANES and hw == fw
    n_experts = w_router.shape[2]
    mp = nb * t
    m = mp + nbs

    tm = _divisor(m, 704, 16)
    tn = _divisor(fw, 512, LANES)
    tq = _divisor(t, 512, LANES)
    chunk = LANES
    group = 8
    ts = _divisor(nbs, 128, 16)
    tp = _divisor(mp, 512, 16)
    hp = 4 if nhg % 4 == 0 and (3 * fw // LANES) % 4 == 0 else (2 if nhg % 2 == 0 else 1)
    pps = _divisor(page_table.shape[1], 4, 1)
    moe_rows = 256 if m * TOP_K // n_experts >= 512 else 16
    tm_out = tm
    tf = _divisor(d, 512, LANES)
    tdn = _divisor(d, 1024, LANES)
    assert nbs % group == 0 and mp % group == 0 and t % chunk == 0

    x = jnp.concatenate([x_prompt.reshape(mp, d), x_sample.reshape(nbs, d)], axis=0)
    w = w_in[0]
    ff0 = 3 * fw
    w_al = jnp.concatenate([w[:, :ff0], w[:, ff0 + nh:],
                            jnp.pad(w[:, ff0:ff0 + nh], ((0, 0), (0, tn - nh)))], axis=1).astype(BF16)
    blk = lambda col: col // LANES
    c_hq, c_hf, c_hi, c_hg = ff0, ff0 + hw, ff0 + 2 * hw, ff0 + 3 * hw
    c_ga = ff0 + 4 * hw
    c_gb = c_ga + d
    c_ff = c_gb + d
    lb = jax.nn.softmax(hg_lower_bound.astype(F32), axis=0)[0:1]
    fbias = jnp.pad(fox_fgate_bias, ((0, 0), (0, LANES - nh)))

    proj = _inproj(x, attn_norm, w_al, tm, tn)
    qn_p, kn_p, vn_p, logf_p = _prep(proj, q_norm, k_norm, fbias, 0, mp, tp, fw, blk(c_ff), LANES)
    qn_s, kn_s, vn_s, logf_s = _prep(proj, q_norm, k_norm, fbias, mp, nbs, ts, fw, blk(c_ff), LANES)

    c = _seq_cumsum(logf_p, nb, t)
    c_t = c[:, :nh].reshape(nb, t, nh).transpose(0, 2, 1).reshape(nb * nh, 1, t)
    o_fox_p = _fox_attn(qn_p, kn_p, vn_p, c, c_t, nb, t, nh, tq, 2 if nh % 2 == 0 else 1)

    lf_new = logf_s[:, :nh].reshape(nbs, 1, nh)
    bias = _page_bias(page_table, jnp.swapaxes(cache_logf, 2, 3), lf_new)
    bias = bias.reshape(nbs, page_table.shape[1], page * nh // LANES, LANES)
    o_fox_s = _paged_attn(page_table, qn_s.reshape(nbs, nh, dh), kn_s.reshape(nbs, nh, dh),
                          vn_s.reshape(nbs, nh, dh), cache_k, cache_v, bias, pps)

    o_hg_p, st_p = _hgrn_prompt(proj, lb, hg_out_norm, nb, t, nhg, blk(c_hq), blk(c_hf), blk(c_hi), blk(c_hg),
                                chunk, hp)
    o_hg_s, st_s = _hgrn_sample(proj, state_hgrn, lb, hg_out_norm, mp // group, nbs, nhg,
                                blk(c_hq), blk(c_hf), blk(c_hi), blk(c_hg), group)

    o_fox = jnp.concatenate([o_fox_p, o_fox_s.reshape(nbs, fw).astype(BF16)], axis=0)
    o_hg = jnp.concatenate([o_hg_p, o_hg_s.astype(BF16)], axis=0)
    merged = _merge(o_fox, o_hg, w_proj_fox[0].astype(BF16), w_proj_hg[0].astype(BF16), proj,
                    c_ga // tn, c_gb // tn, tm, tn)

    ne_pad = -(-n_experts // LANES) * LANES
    w_r = jnp.pad(w_router[0], ((0, 0), (0, ne_pad - n_experts))).astype(BF16)
    b_r = jnp.pad(b_router, ((0, 0), (0, ne_pad - n_experts)))
    h, hn, logits = _outproj(merged, w_out[0].astype(BF16), x, ffn_norm, w_r, b_r, tm_out, tn)

    pos, gates, row_token, sched = _route(logits[:, :n_experts], n_experts, moe_rows)
    xs = _gather_rows(row_token, (sched[2] > 0).astype(jnp.int32), hn, moe_rows, d // 2 // LANES)
    act = _moe_up(sched, xs, w_gate_up.reshape(w_gate_up.shape[1:]), b_gate_up.reshape(n_experts, 1, -1),
                  moe_rows, tf)
    y_rows = _moe_down(sched, act, w_down.reshape(w_down.shape[1:]), b_down.reshape(n_experts, 1, -1),
                       moe_rows, tdn)
    h2, hn2 = _combine(pos, h, gates, ple_norm, y_rows, _divisor(m, 128, 16))

    w_pg = w_ple_gate[0].astype(BF16)
    w_pp = w_ple_proj[0].astype(BF16)
    y_p = _ple(hn2, w_pg, p_prompt[0].reshape(mp, -1).astype(BF16), w_pp, h2, 0, mp, tp, tn)
    y_s = _ple(hn2, w_pg, p_sample[0].reshape(nbs, -1).astype(BF16), w_pp, h2, mp, nbs, ts, tn)

    return (y_p.reshape(nb, t, d), y_s.reshape(nbs, 1, d),
            kn_p.reshape(1, nb, t, nh, dh), vn_p.reshape(1, nb, t, nh, dh), logf_p[:, :nh].reshape(1, nb, t, nh),
            st_p[None],
            kn_s.reshape(1, nbs, 1, nh, dh), vn_s.reshape(1, nbs, 1, nh, dh), logf_s[:, :nh].reshape(1, nbs, 1, nh),
            st_s)
```
